```python
import functools
import jax
import jax.numpy as jnp
from jax import lax
import numpy as np

D_MODEL = 1024
BATCH = 4
SEQ = 4096
DEPTH = 2
DEC_BATCH = 128
DEC_SEQ = 8
PAST_LEN = 16384
PAGE_SIZE = 128

N_HEADS = 8
Q_LORA = 384
KV_LORA = 256
QK_NOPE = 64
QK_ROPE = 32
QK_DIM = QK_NOPE + QK_ROPE
V_DIM = 64
KV_ROW = KV_LORA + QK_ROPE
ROPE_THETA = 10000.0
Q_BLOCK = 128
LRU_WIDTH = 512
LRU_HEADS = 8
LRU_HD = LRU_WIDTH // LRU_HEADS
CONV_WIDTH = 4
RG_C = 8.0
GM_WIDTH = 512
GM_GROUPS = 8
GM_GD = GM_WIDTH // GM_GROUPS
CHUNK = 128
N_BRANCH = 3
BRANCH_W = N_HEADS * V_DIM
IN_COLS = Q_LORA + KV_LORA + QK_ROPE + 2 * LRU_WIDTH + 2 * GM_WIDTH + N_BRANCH * D_MODEL
PEER_HEADS = 8
N_KEYS = 128
N_EXPERTS = N_KEYS * N_KEYS
PEER_QDIM = 256
PEER_HALF = PEER_QDIM // 2
PEER_TOPK = 16
PEER_BLOCK = 128
EPS = 1e-6

kernel_name = 'hybrid_mla_rglru_gmlp_peer_adaln_step'


def _rmsnorm(x, g):
    xf = x.astype(jnp.float32)
    y = xf * lax.rsqrt(jnp.mean(xf * xf, axis=-1, keepdims=True) + EPS)
    return (y * g.astype(jnp.float32)).astype(x.dtype)


def _layernorm(x, g, b):
    xf = x.astype(jnp.float32)
    xc = xf - jnp.mean(xf, axis=-1, keepdims=True)
    y = xc * lax.rsqrt(jnp.mean(xc * xc, axis=-1, keepdims=True) + EPS)
    return (y * g.astype(jnp.float32) + b.astype(jnp.float32)).astype(x.dtype)


def _gelu(x):
    return jax.nn.gelu(x, approximate=False)


def _rope(x, pos):
    half = x.shape[-1] // 2
    freq = ROPE_THETA ** (-jnp.arange(half, dtype=jnp.float32) / half)
    ang = pos.astype(jnp.float32)[:, None] * freq[None, :]
    ang = ang.reshape((ang.shape[0],) + (1,) * (x.ndim - 3) + (half,))
    cos, sin = jnp.cos(ang), jnp.sin(ang)
    xf = x.astype(jnp.float32)
    x1, x2 = xf[..., :half], xf[..., half:]
    return jnp.concatenate([x1 * cos - x2 * sin, x2 * cos + x1 * sin], axis=-1).astype(x.dtype)


def _mla_keys(P, l, rows):
    lat, kr = rows[..., :KV_LORA], rows[..., KV_LORA:]
    lead = rows.shape[:-1]
    k_nope = (lat @ P['w_uk'][l]).reshape(lead + (N_HEADS, QK_NOPE))
    k_rope = jnp.broadcast_to(kr[..., None, :], lead + (N_HEADS, QK_ROPE))
    return _rmsnorm(jnp.concatenate([k_nope, k_rope], axis=-1), P['g_kn'][l])


def _mla_attend(q, k, lat, mask, w_uv):
    s = jnp.einsum('qhd,khd->hqk', q, k, preferred_element_type=jnp.float32) * (QK_DIM ** -0.5)
    s = jnp.where(mask[None], s, -1e30)
    p = jax.nn.softmax(s, axis=-1).astype(lat.dtype)
    ctx = jnp.einsum('hqk,kc->qhc', p, lat)
    out = jnp.einsum('qhc,chv->qhv', ctx, w_uv.reshape(KV_LORA, N_HEADS, V_DIM))
    return out.reshape(q.shape[0], N_HEADS * V_DIM)


def _attend_prompt(P, l, q, rows):
    B, T = q.shape[0], q.shape[1]
    k = _mla_keys(P, l, rows)
    lat = rows[..., :KV_LORA]
    nb = T // Q_BLOCK
    qb = q.reshape(B, nb, Q_BLOCK, N_HEADS, QK_DIM).transpose(1, 0, 2, 3, 4)
    key_pos = jnp.arange(T)
    w_uv = P['w_uv'][l]

    def block(args):
        bi, qi = args
        q_pos = bi * Q_BLOCK + jnp.arange(Q_BLOCK)
        mask = q_pos[:, None] >= key_pos[None, :]
        return jax.vmap(_mla_attend, in_axes=(0, 0, 0, None, None))(qi, k, lat, mask, w_uv)

    out = lax.map(block, (jnp.arange(nb), qb))
    return out.transpose(1, 0, 2, 3).reshape(B, T, N_HEADS * V_DIM)


def _attend_sample(P, l, cache_kv, page_table, q, rows):
    w_uv = P['w_uv'][l]

    def one(args):
        pt, qi, ri = args
        past = cache_kv[pt, l].reshape(-1, KV_ROW).astype(ri.dtype)
        allr = jnp.concatenate([past, ri], axis=0)
        n_past = past.shape[0]
        k = _mla_keys(P, l, allr)
        mask = jnp.arange(allr.shape[0])[None, :] <= n_past + jnp.arange(ri.shape[0])[:, None]
        return _mla_attend(qi, k, allr[:, :KV_LORA], mask, w_uv)

    return lax.map(one, (page_table, q, rows))


def _lin_combine(e1, e2):
    a1, b1 = e1
    a2, b2 = e2
    return a1 * a2, a2 * b1 + b2


def _rglru(P, l, zx, zy, conv_buf, h0):
    B, T, W = zx.shape
    xpad = jnp.concatenate([conv_buf.astype(zx.dtype), zx], axis=1)
    rhs = P['conv_w'][l].astype(zx.dtype)[:, None, :]
    xc = lax.conv_general_dilated(xpad, rhs, (1,), 'VALID', dimension_numbers=('NWC', 'WIO', 'NWC'),
                                  feature_group_count=W) + P['conv_b'][l]
    xh = xc.reshape(B, T, LRU_HEADS, LRU_HD)
    r = jax.nn.sigmoid(jnp.einsum('bthi,hij->bthj', xh, P['w_ra'][l]).reshape(B, T, W) + P['b_ra'][l])
    i = jax.nn.sigmoid(jnp.einsum('bthi,hij->bthj', xh, P['w_ri'][l]).reshape(B, T, W) + P['b_ri'][l])
    log_a = -RG_C * r.astype(jnp.float32) * jax.nn.softplus(-P['lru_lambda'][l].astype(jnp.float32))
    a = jnp.exp(log_a)
    bterm = jnp.sqrt(-jnp.expm1(2.0 * log_a)) * (i * xc).astype(jnp.float32)
    A, Bc = lax.associative_scan(_lin_combine, (a, bterm), axis=1)
    h = A * h0.astype(jnp.float32)[:, None, :] + Bc
    out = h.astype(zx.dtype) * _gelu(zy)
    return out, h[:, -1].astype(zx.dtype), xpad[:, -(CONV_WIDTH - 1):]


def _gmlp(P, l, u, v, chunk_len):
    B, T, _ = v.shape
    vn = _layernorm(v, P['gm_ln_g'][l], P['gm_ln_b'][l])
    nc = T // chunk_len
    vc = vn.reshape(B, nc, chunk_len, GM_GROUPS, GM_GD)
    w = jnp.tril(P['w_s'][l][:, :chunk_len, :chunk_len])
    b = P['b_s'][l][:, :chunk_len].T
    mixed = jnp.einsum('gts,bcsgd->bctgd', w, vc) + b[None, None, :, :, None]
    return u * mixed.reshape(B, T, GM_WIDTH), vn


def _peer(P, l, h):
    B, T, D = h.shape
    w_pq, sub_keys = P['w_pq'][l], P['sub_keys'][l]
    eu, ev = P['expert_u'][l], P['expert_v'][l]
    n = B * T
    nb = -(-n // PEER_BLOCK)
    flat = jnp.pad(h.reshape(n, D), ((0, nb * PEER_BLOCK - n), (0, 0))).reshape(nb, PEER_BLOCK, D)
    kk = PEER_TOPK * PEER_TOPK

    def block(hb):
        q = (hb @ w_pq).reshape(PEER_BLOCK, PEER_HEADS, 2, PEER_HALF)
        s = jnp.einsum('thpd,hpkd->thpk', q, sub_keys, preferred_element_type=jnp.float32)
        s1, i1 = lax.top_k(s[:, :, 0], PEER_TOPK)
        s2, i2 = lax.top_k(s[:, :, 1], PEER_TOPK)
        cand = (s1[..., :, None] + s2[..., None, :]).reshape(PEER_BLOCK, PEER_HEADS, kk)
        cidx = (i1[..., :, None] * N_KEYS + i2[..., None, :]).reshape(PEER_BLOCK, PEER_HEADS, kk)
        best, sel = lax.top_k(cand, PEER_TOPK)
        eidx = jnp.take_along_axis(cidx, sel, axis=-1)
        g = jax.nn.softmax(best, axis=-1)
        act = _gelu(jnp.einsum('td,thkd->thk', hb, eu[eidx]))
        w = (g * act.astype(jnp.float32)).astype(hb.dtype)
        return jnp.einsum('thk,thkd->td', w, ev[eidx])

    out = lax.map(block, flat).reshape(nb * PEER_BLOCK, D)[:n]
    return out.reshape(B, T, D)


def _layer(P, l, x, c, pos, attend, conv_buf, h0, chunk_len):
    B, T, D = x.shape
    mod = jax.nn.silu(c) @ P['w_ada'][l] + P['b_ada'][l]
    sh1, sc1, gt1, sh2, sc2, gt2 = jnp.split(mod[:, None, :], 6, axis=-1)
    h = _rmsnorm(x, P['g_norm_mix'][l]) * (1 + sc1) + sh1
    z = h @ P['w_in'][l]
    cuts = np.cumsum([Q_LORA, KV_LORA, QK_ROPE, LRU_WIDTH, LRU_WIDTH, GM_WIDTH, GM_WIDTH]).tolist()
    zq, zkv, zkr, zx, zy, zu, zv, zg = jnp.split(z, cuts, axis=-1)
    q = (_rmsnorm(zq, P['g_q_lora'][l]) @ P['w_uq'][l]).reshape(B, T, N_HEADS, QK_DIM)
    q = jnp.concatenate([q[..., :QK_NOPE], _rope(q[..., QK_NOPE:], pos)], axis=-1)
    q = _rmsnorm(q, P['g_qn'][l])
    rows = jnp.concatenate([_rmsnorm(zkv, P['g_kv_lora'][l]), _rope(zkr, pos)], axis=-1)
    o_a = attend(q, rows)
    o_b, h_last, buf = _rglru(P, l, zx, zy, conv_buf, h0)
    o_c, vn = _gmlp(P, l, _gelu(zu), _gelu(zv), chunk_len)
    br = jnp.einsum('btkw,kwd->btkd', jnp.stack([o_a, o_b, o_c], axis=2), P['w_branch'][l])
    gates = jax.nn.sigmoid(zg).reshape(B, T, N_BRANCH, D)
    merged = jnp.sum(gates * br, axis=2)
    x = x + gt1 * (merged @ P['w_out'][l])
    h2 = _rmsnorm(x, P['g_norm_ffn'][l]) * (1 + sc2) + sh2
    x = x + gt2 * _peer(P, l, h2)
    return x, rows, h_last, buf, vn


def setup_inputs(seed: int = 0) -> dict:
    key = jax.random.key(seed)
    keys = list(jax.random.split(key, 64))

    def nrm(shape, scale):
        return jax.random.normal(keys.pop(), shape, jnp.float32) * scale

    n_pages = PAST_LEN // PAGE_SIZE
    n_pool = (DEC_BATCH * n_pages * 5) // 4
    D = D_MODEL
    x_prompt = nrm((BATCH, SEQ, D), 1.0)
    x_sample = nrm((DEC_BATCH, DEC_SEQ, D), 1.0)
    c_prompt = nrm((BATCH, D), 1.0)
    c_sample = nrm((DEC_BATCH, D), 1.0)
    cache_kv = nrm((n_pool, DEPTH, PAGE_SIZE, KV_ROW), 1.0)
    state_lru_h = nrm((DEC_BATCH, DEPTH, LRU_WIDTH), 0.5)
    state_conv = nrm((DEC_BATCH, DEPTH, CONV_WIDTH - 1, LRU_WIDTH), 1.0)
    perm = jax.random.permutation(keys.pop(), n_pool)
    page_table = perm[:DEC_BATCH * n_pages].reshape(DEC_BATCH, n_pages).astype(jnp.int32)
    u = jax.random.uniform(keys.pop(), (DEPTH, LRU_WIDTH), jnp.float32, 0.9, 0.999)
    sgm = u ** (1.0 / RG_C)
    lru_lambda = jnp.log(sgm) - jnp.log1p(-sgm)
    return {
        'x_prompt': x_prompt, 'x_sample': x_sample, 'c_prompt': c_prompt, 'c_sample': c_sample,
        'cache_kv': cache_kv, 'state_lru_h': state_lru_h, 'state_conv': state_conv, 'page_table': page_table,
        'w_ada': nrm((DEPTH, D, 6 * D), 0.5 * D ** -0.5),
        'b_ada': nrm((DEPTH, 6 * D), 0.01),
        'g_norm_mix': 1.0 + nrm((DEPTH, D), 0.02),
        'g_norm_ffn': 1.0 + nrm((DEPTH, D), 0.02),
        'w_in': nrm((DEPTH, D, IN_COLS), D ** -0.5),
        'g_q_lora': 1.0 + nrm((DEPTH, Q_LORA), 0.02),
        'w_uq': nrm((DEPTH, Q_LORA, N_HEADS * QK_DIM), Q_LORA ** -0.5),
        'g_kv_lora': 1.0 + nrm((DEPTH, KV_LORA), 0.02),
        'w_uk': nrm((DEPTH, KV_LORA, N_HEADS * QK_NOPE), KV_LORA ** -0.5),
        'w_uv': nrm((DEPTH, KV_LORA, N_HEADS * V_DIM), KV_LORA ** -0.5),
        'g_qn': 1.0 + nrm((DEPTH, QK_DIM), 0.02),
        'g_kn': 1.0 + nrm((DEPTH, QK_DIM), 0.02),
        'conv_w': nrm((DEPTH, CONV_WIDTH, LRU_WIDTH), CONV_WIDTH ** -0.5),
        'conv_b': nrm((DEPTH, LRU_WIDTH), 0.01),
        'w_ra': nrm((DEPTH, LRU_HEADS, LRU_HD, LRU_HD), LRU_HD ** -0.5),
        'b_ra': nrm((DEPTH, LRU_WIDTH), 0.01),
        'w_ri': nrm((DEPTH, LRU_HEADS, LRU_HD, LRU_HD), LRU_HD ** -0.5),
        'b_ri': nrm((DEPTH, LRU_WIDTH), 0.01),
        'lru_lambda': lru_lambda,
        'gm_ln_g': 1.0 + nrm((DEPTH, GM_WIDTH), 0.02),
        'gm_ln_b': nrm((DEPTH, GM_WIDTH), 0.01),
        'w_s': nrm((DEPTH, GM_GROUPS, CHUNK, CHUNK), CHUNK ** -0.5),
        'b_s': 1.0 + nrm((DEPTH, GM_GROUPS, CHUNK), 0.02),
        'w_branch': nrm((DEPTH, N_BRANCH, BRANCH_W, D), BRANCH_W ** -0.5),
        'w_out': nrm((DEPTH, D, D), D ** -0.5),
        'w_pq': nrm((DEPTH, D, PEER_HEADS * PEER_QDIM), D ** -0.5),
        'sub_keys': nrm((DEPTH, PEER_HEADS, 2, N_KEYS, PEER_HALF), PEER_HALF ** -0.5),
        'expert_u': nrm((DEPTH, N_EXPERTS, D), D ** -0.5),
        'expert_v': nrm((DEPTH, N_EXPERTS, D), 0.5),
    }


def reference(x_prompt, x_sample, c_prompt, c_sample, cache_kv, state_lru_h, state_conv, page_table,
              w_ada, b_ada, g_norm_mix, g_norm_ffn, w_in, g_q_lora, w_uq, g_kv_lora, w_uk, w_uv, g_qn, g_kn,
              conv_w, conv_b, w_ra, b_ra, w_ri, b_ri, lru_lambda, gm_ln_g, gm_ln_b, w_s, b_s,
              w_branch, w_out, w_pq, sub_keys, expert_u, expert_v):
    P = dict(w_ada=w_ada, b_ada=b_ada, g_norm_mix=g_norm_mix, g_norm_ffn=g_norm_ffn, w_in=w_in,
             g_q_lora=g_q_lora, w_uq=w_uq, g_kv_lora=g_kv_lora, w_uk=w_uk, w_uv=w_uv, g_qn=g_qn, g_kn=g_kn,
             conv_w=conv_w, conv_b=conv_b, w_ra=w_ra, b_ra=b_ra, w_ri=w_ri, b_ri=b_ri, lru_lambda=lru_lambda,
             gm_ln_g=gm_ln_g, gm_ln_b=gm_ln_b, w_s=w_s, b_s=b_s, w_branch=w_branch, w_out=w_out,
             w_pq=w_pq, sub_keys=sub_keys, expert_u=expert_u, expert_v=expert_v)
    Bp, Tp = x_prompt.shape[0], x_prompt.shape[1]
    Ts = x_sample.shape[1]
    n_past = page_table.shape[1] * cache_kv.shape[2]
    pos_p = jnp.arange(Tp, dtype=jnp.float32)
    pos_s = n_past + jnp.arange(Ts, dtype=jnp.float32)
    buf0 = jnp.zeros((Bp, CONV_WIDTH - 1, LRU_WIDTH), x_prompt.dtype)
    h00 = jnp.zeros((Bp, LRU_WIDTH), x_prompt.dtype)
    xp, xs = x_prompt, x_sample
    kv_p, kv_s, hp_l, hs_l, cp_l, cs_l, vp_l, vs_l = [], [], [], [], [], [], [], []
    for l in range(DEPTH):
        xp, rp, hp, cp, vp = _layer(P, l, xp, c_prompt, pos_p, functools.partial(_attend_prompt, P, l),
                                    buf0, h00, CHUNK)
        xs, rs, hs, cs, vs = _layer(P, l, xs, c_sample, pos_s,
                                    functools.partial(_attend_sample, P, l, cache_kv, page_table),
                                    state_conv[:, l], state_lru_h[:, l], Ts)
        kv_p.append(rp)
        kv_s.append(rs)
        hp_l.append(hp)
        hs_l.append(hs)
        cp_l.append(cp)
        cs_l.append(cs)
        vp_l.append(vp[:, -CHUNK:])
        vs_l.append(vs)
    new_kv_prompt = jnp.stack(kv_p, axis=1)
    new_kv_sample = jnp.stack(kv_s, axis=1)
    lru_h_prompt = jnp.stack(hp_l, axis=1)
    lru_h_sample = jnp.stack(hs_l, axis=1)
    conv_prompt = jnp.stack(cp_l, axis=1)
    conv_sample = jnp.stack(cs_l, axis=1)
    gm_v_prompt = jnp.stack(vp_l, axis=1)
    gm_v_sample = jnp.stack(vs_l, axis=1)
    return (xp, xs, new_kv_prompt, new_kv_sample, lru_h_prompt, lru_h_sample, conv_prompt, conv_sample, gm_v_prompt, gm_v_sample)
```

```python
import functools
import math

import numpy as np
import jax
import jax.numpy as jnp
from jax import lax
from jax.experimental import pallas as pl
from jax.experimental.pallas import tpu as pltpu

F32 = jnp.float32
BF16 = jnp.bfloat16

D_MODEL = 1024
DEPTH = 2
N_HEADS = 8
Q_LORA = 384
KV_LORA = 256
QK_NOPE = 64
QK_ROPE = 32
QK_DIM = QK_NOPE + QK_ROPE
V_DIM = 64
KV_ROW = KV_LORA + QK_ROPE
ROPE_THETA = 10000.0
LRU_WIDTH = 512
LRU_HEADS = 8
LRU_HD = LRU_WIDTH // LRU_HEADS
CONV_WIDTH = 4
RG_C = 8.0
GM_WIDTH = 512
GM_GROUPS = 8
GM_GD = GM_WIDTH // GM_GROUPS
CHUNK = 128
PEER_HEADS = 8
N_KEYS = 128
PEER_HALF = 128
PEER_TOPK = 16
EPS = 1e-6
NEG = -1e30

LANES = 128
SUBLANES = 8
HEAD_PAD = LANES
VMEM_LIMIT = 56 * 1024 * 1024

NT_DIMS = (((1,), (1,)), ((), ()))


def _cparams(sem):
    return pltpu.CompilerParams(dimension_semantics=sem, vmem_limit_bytes=VMEM_LIMIT)


def _resident(shape):
    nd = len(shape)
    return pl.BlockSpec(shape, lambda *_: (0,) * nd, pipeline_mode=pl.Buffered(1))


def _gelu(x):
    return 0.5 * x * (1.0 + lax.erf(x * (2.0 ** -0.5)))


def _sigmoid(x):
    return jax.nn.sigmoid(x)


def _rms(x, g, n=None):
    n = x.shape[-1] if n is None else n
    ss = jnp.sum(x * x, axis=-1, keepdims=True)
    return x * lax.rsqrt(ss * (1.0 / n) + EPS) * g


def _ada_kernel(c_ref, w_ref, b_ref, o_ref):
    c = c_ref[...]
    s = (c * _sigmoid(c)).astype(BF16)
    o_ref[0] = jnp.dot(s, w_ref[0].astype(BF16), preferred_element_type=F32) + b_ref[0]


def _ada(c_all, w_ada, b_ada):
    nb = c_all.shape[0]
    tn = 512
    ncol = w_ada.shape[-1]
    return pl.pallas_call(
        _ada_kernel,
        grid=(DEPTH, ncol // tn),
        in_specs=[
            pl.BlockSpec((nb, D_MODEL), lambda l, j: (0, 0)),
            pl.BlockSpec((1, D_MODEL, tn), lambda l, j: (l, 0, j)),
            pl.BlockSpec((1, 1, tn), lambda l, j: (l, 0, j)),
        ],
        out_specs=pl.BlockSpec((1, nb, tn), lambda l, j: (l, 0, j)),
        out_shape=jax.ShapeDtypeStruct((DEPTH, nb, ncol), F32),
        compiler_params=_cparams(("arbitrary", "arbitrary")),
        name="ada",
    )(c_all, w_ada, b_ada.reshape(DEPTH, 1, ncol))


def _mod_spec(mod, tm):
    if mod.shape[1] == 1:
        return pl.BlockSpec((1, 1, mod.shape[2]), lambda g, i: (g, 0, 0))
    return pl.BlockSpec((1, tm, mod.shape[2]), lambda g, i: (g, i, 0))


Z_SPLITS = (Q_LORA, 384, 2 * LRU_WIDTH, 2 * GM_WIDTH, 3 * D_MODEL)
Z_COLS = sum(Z_SPLITS)


def _inproj_kernel(x_ref, sc_ref, sh_ref, g_ref, w_ref, *o_refs):
    x = x_ref[0]
    h = _rms(x, g_ref[...]) * (1.0 + sc_ref[0]) + sh_ref[0]
    hb = h.astype(BF16)
    off = 0
    for o_ref, width in zip(o_refs, Z_SPLITS):
        o_ref[0] = jnp.dot(hb, w_ref[:, off:off + width], preferred_element_type=F32)
        off += width


def _inproj(x, sc, sh, g, w_in_p, tm):
    G, Tg, D = x.shape
    out_shape = [jax.ShapeDtypeStruct((G, Tg, w), F32) for w in Z_SPLITS]
    out_specs = [pl.BlockSpec((1, tm, w), lambda g_, i: (g_, i, 0)) for w in Z_SPLITS]
    return pl.pallas_call(
        _inproj_kernel,
        grid=(G, Tg // tm),
        in_specs=[
            pl.BlockSpec((1, tm, D), lambda g_, i: (g_, i, 0)),
            _mod_spec(sc, tm), _mod_spec(sh, tm),
            _resident((1, D)), _resident((D, Z_COLS)),
        ],
        out_specs=out_specs,
        out_shape=out_shape,
        compiler_params=_cparams(("arbitrary", "arbitrary")),
        name="inproj",
    )(x, sc, sh, g, w_in_p)


def _rope_block(blk, c, sa, sb):
    n = blk.shape[-1]
    return blk * c + pltpu.roll(blk, 16, 1) * sa + pltpu.roll(blk, n - 16, 1) * sb


def _qkprep_kernel(zq_ref, zkvr_ref, c_ref, sa_ref, sb_ref, gq_ref, wuq_ref, gkv_ref, wuk_ref,
                   wuv_ref, gqn_ref, gkn_ref, q_ref, k_ref, v_ref, rows_ref):
    c, sa, sb = c_ref[...], sa_ref[...], sb_ref[...]
    qn = _rms(zq_ref[0], gq_ref[...]).astype(BF16)
    qf = jnp.dot(qn, wuq_ref[...], preferred_element_type=F32)
    gqn, gkn = gqn_ref[...], gkn_ref[...]
    for h in range(N_HEADS):
        sl = slice(h * HEAD_PAD, (h + 1) * HEAD_PAD)
        r = _rope_block(qf[:, sl], c, sa, sb)
        q_ref[0, :, sl] = _rms(r, gqn, QK_DIM).astype(q_ref.dtype)
    zkvr = zkvr_ref[0]
    lat = _rms(zkvr[:, :KV_LORA], gkv_ref[...])
    krb = _rope_block(zkvr[:, KV_LORA:], c, sa, sb)
    rows_ref[0, :, 0:KV_LORA] = lat
    rows_ref[0, :, KV_LORA:KV_ROW] = krb[:, QK_NOPE:QK_DIM]
    latb = lat.astype(BF16)
    knp = jnp.dot(latb, wuk_ref[...], preferred_element_type=F32)
    for h in range(N_HEADS):
        sl = slice(h * HEAD_PAD, (h + 1) * HEAD_PAD)
        k_ref[0, :, sl] = _rms(knp[:, sl] + krb, gkn, QK_DIM).astype(BF16)
    v_ref[0] = jnp.dot(latb, wuv_ref[...], preferred_element_type=F32).astype(BF16)


def _qkprep(zq, zkvr, tabs, W, tm, q_dtype):
    G, Tg, _ = zq.shape
    c, sa, sb = tabs
    HP = N_HEADS * HEAD_PAD
    tok = lambda w: pl.BlockSpec((1, tm, w), lambda g_, i: (g_, i, 0))
    tab = pl.BlockSpec((tm, HEAD_PAD), lambda g_, i: (i, 0))
    return pl.pallas_call(
        _qkprep_kernel,
        grid=(G, Tg // tm),
        in_specs=[tok(Q_LORA), tok(384), tab, tab, tab,
                  _resident((1, Q_LORA)), _resident((Q_LORA, HP)), _resident((1, KV_LORA)),
                  _resident((KV_LORA, HP)), _resident((KV_LORA, HP)),
                  _resident((1, HEAD_PAD)), _resident((1, HEAD_PAD))],
        out_specs=[tok(HP), tok(HP), tok(HP), tok(KV_ROW)],
        out_shape=[jax.ShapeDtypeStruct((G, Tg, HP), q_dtype),
                   jax.ShapeDtypeStruct((G, Tg, HP), BF16),
                   jax.ShapeDtypeStruct((G, Tg, HP), BF16),
                   jax.ShapeDtypeStruct((G, Tg, KV_ROW), F32)],
        compiler_params=_cparams(("arbitrary", "arbitrary")),
        name="qkprep",
    )(zq, zkvr, c, sa, sb, W["g_q"], W["w_uq_p"], W["g_kv"], W["w_uk_p"], W["w_uv_p"],
      W["g_qn_p"], W["g_kn_p"])


def _flash_kernel(q_ref, k_ref, v_ref, o_ref, m_ref, l_ref, acc_ref, *, tq, tk):
    qi, ki = pl.program_id(1), pl.program_id(2)
    scale = QK_DIM ** -0.5

    @pl.when(ki == 0)
    def _():
        m_ref[...] = jnp.full(m_ref.shape, NEG, F32)
        l_ref[...] = jnp.zeros(l_ref.shape, F32)
        acc_ref[...] = jnp.zeros(acc_ref.shape, F32)

    @pl.when(ki <= qi)
    def _():
        rowpos = qi * tq + lax.broadcasted_iota(jnp.int32, (tq, tk), 0)
        colpos = ki * tk + lax.broadcasted_iota(jnp.int32, (tq, tk), 1)
        causal = rowpos >= colpos
        for h in range(N_HEADS):
            sl = slice(h * HEAD_PAD, (h + 1) * HEAD_PAD)
            s = lax.dot_general(q_ref[0, :, sl], k_ref[0, :, sl], NT_DIMS,
                                preferred_element_type=F32) * scale
            s = jnp.where(causal, s, NEG)
            m_prev = m_ref[h]
            m_new = jnp.maximum(m_prev, jnp.max(s, axis=-1, keepdims=True))
            alpha = jnp.exp(m_prev - m_new)
            p = jnp.exp(s - m_new)
            l_ref[h] = alpha * l_ref[h] + jnp.sum(p, axis=-1, keepdims=True)
            acc_ref[h] = alpha * acc_ref[h] + jnp.dot(p.astype(BF16), v_ref[0, :, sl],
                                                      preferred_element_type=F32)
            m_ref[h] = m_new

    @pl.when(ki == qi)
    def _():
        for h in range(N_HEADS):
            sl = slice(h * HEAD_PAD, (h + 1) * HEAD_PAD)
            o_ref[0, :, sl] = (acc_ref[h] / l_ref[h]).astype(o_ref.dtype)


def _flash(q, k, v, tq=512):
    B, T, HP = q.shape
    tk = tq
    kv_spec = pl.BlockSpec((1, tk, HP), lambda b, i, j: (b, jnp.minimum(i, j), 0))
    return pl.pallas_call(
        functools.partial(_flash_kernel, tq=tq, tk=tk),
        grid=(B, T // tq, T // tk),
        in_specs=[pl.BlockSpec((1, tq, HP), lambda b, i, j: (b, i, 0)), kv_spec, kv_spec],
        out_specs=pl.BlockSpec((1, tq, HP), lambda b, i, j: (b, i, 0)),
        out_shape=jax.ShapeDtypeStruct((B, T, HP), BF16),
        scratch_shapes=[pltpu.VMEM((N_HEADS, tq, 1), F32), pltpu.VMEM((N_HEADS, tq, 1), F32),
                        pltpu.VMEM((N_HEADS, tq, HEAD_PAD), F32)],
        compiler_params=_cparams(("arbitrary", "arbitrary", "arbitrary")),
        name="flash",
    )(q, k, v)


def _qabs_kernel(q_ref, gkn_ref, wuk_ref, qabs_ref, qg_ref):
    g = gkn_ref[...]
    for h in range(N_HEADS):
        sl = slice(h * HEAD_PAD, (h + 1) * HEAD_PAD)
        qg = q_ref[:, sl] * g
        qg_ref[:, sl] = qg
        qabs_ref[:, h * KV_LORA:(h + 1) * KV_LORA] = lax.dot_general(
            qg, wuk_ref[:, sl], NT_DIMS, preferred_element_type=F32,
            precision=lax.Precision.HIGHEST)


def _qabs(q2d, W):
    n, HP = q2d.shape
    tm = 256
    return pl.pallas_call(
        _qabs_kernel,
        grid=(n // tm,),
        in_specs=[pl.BlockSpec((tm, HP), lambda i: (i, 0)), _resident((1, HEAD_PAD)),
                  _resident((KV_LORA, HP))],
        out_specs=[pl.BlockSpec((tm, N_HEADS * KV_LORA), lambda i: (i, 0)),
                   pl.BlockSpec((tm, HP), lambda i: (i, 0))],
        out_shape=[jax.ShapeDtypeStruct((n, N_HEADS * KV_LORA), F32),
                   jax.ShapeDtypeStruct((n, HP), F32)],
        compiler_params=_cparams(("arbitrary",)),
        name="qabs",
    )(q2d, W["g_kn_p"], W["w_uk_pf"])


PAGES_PER_STEP = 16


def _paged_kernel(pt_ref, qabs_ref, qr_ref, new_ref, wukT_ref, wuv_ref, cache_ref, o_ref,
                  buf, sem, lhs, m_ref, l_ref, acc_ref, newbuf, *, layer, n_seq, n_chunks, page):
    P = PAGES_PER_STEP
    s_id, c_id = pl.program_id(0), pl.program_id(1)
    step = s_id * n_chunks + c_id
    total = n_seq * n_chunks
    nq = new_ref.shape[1]
    HQ = N_HEADS * nq
    scale = QK_DIM ** -0.5

    def page_copy(slot, p, page_id):
        return pltpu.make_async_copy(cache_ref.at[page_id, layer], buf.at[slot, p], sem.at[slot, p])

    def start_fetch(step_, slot):
        for p in range(P):
            page_copy(slot, p, pt_ref[step_ * P + p]).start()

    def wait_fetch(slot):
        for p in range(P):
            page_copy(slot, p, 0).wait()

    @pl.when(step == 0)
    def _():
        lhs[0:N_HEADS * QK_NOPE, :] = wukT_ref[...]
        start_fetch(0, 0)

    @pl.when(step + 1 < total)
    def _():
        start_fetch(step + 1, (step + 1) % 2)

    @pl.when(c_id == 0)
    def _():
        lhs[N_HEADS * QK_NOPE:, :] = qabs_ref[0]
        m_ref[...] = jnp.full(m_ref.shape, NEG, F32)
        l_ref[...] = jnp.zeros(l_ref.shape, F32)
        acc_ref[...] = jnp.zeros(acc_ref.shape, F32)

    ones = jnp.ones((SUBLANES, QK_ROPE), BF16)

    def process(rows, mask):
        lat = rows[:, :KV_LORA].astype(BF16)
        kr = rows[:, KV_LORA:KV_ROW]
        big = lax.dot_general(lhs[...], lat, NT_DIMS, preferred_element_type=F32)
        sc = big[N_HEADS * QK_NOPE:] + lax.dot_general(qr_ref[0], kr.astype(BF16), NT_DIMS,
                                                        preferred_element_type=F32)
        k2 = kr * kr
        k2h = k2.astype(BF16)
        k2l = (k2 - k2h.astype(F32)).astype(BF16)
        ssk = (lax.dot_general(ones, k2h, NT_DIMS, preferred_element_type=F32)
               + lax.dot_general(ones, k2l, NT_DIMS, preferred_element_type=F32))[0:1]
        parts = []
        for h in range(N_HEADS):
            kn = big[h * QK_NOPE:(h + 1) * QK_NOPE]
            ss = jnp.sum(kn * kn, axis=0, keepdims=True) + ssk
            rs = lax.rsqrt(ss * (1.0 / QK_DIM) + EPS)
            parts.append(sc[h * nq:(h + 1) * nq] * rs)
        s = jnp.concatenate(parts, axis=0) * scale
        if mask is not None:
            s = jnp.where(mask, s, NEG)
        m_prev = m_ref[...]
        m_new = jnp.maximum(m_prev, jnp.max(s, axis=-1, keepdims=True))
        alpha = jnp.exp(m_prev - m_new)
        p = jnp.exp(s - m_new)
        l_ref[...] = alpha * l_ref[...] + jnp.sum(p, axis=-1, keepdims=True)
        acc_ref[...] = alpha * acc_ref[...] + jnp.dot(p.astype(BF16), lat, preferred_element_type=F32)
        m_ref[...] = m_new

    slot = step % 2
    wait_fetch(slot)
    process(buf[slot].reshape(P * page, KV_ROW), None)

    @pl.when(c_id == n_chunks - 1)
    def _():
        newbuf[...] = jnp.zeros(newbuf.shape, F32)
        newbuf[0:nq, :] = new_ref[0]
        key = lax.broadcasted_iota(jnp.int32, (HQ, newbuf.shape[0]), 1)
        qpos = lax.broadcasted_iota(jnp.int32, (HQ, newbuf.shape[0]), 0) % nq
        process(newbuf[...], key <= qpos)
        for h in range(N_HEADS):
            ctx = (acc_ref[h * nq:(h + 1) * nq] / l_ref[h * nq:(h + 1) * nq]).astype(BF16)
            sl = slice(h * HEAD_PAD, (h + 1) * HEAD_PAD)
            o_ref[0, :, sl] = jnp.dot(ctx, wuv_ref[:, sl], preferred_element_type=F32)


def _paged(page_table, qabs, qr, new_rows, cache_kv, W, layer):
    n_seq, n_pages = page_table.shape
    page = cache_kv.shape[2]
    nq = new_rows.shape[1]
    HQ = N_HEADS * nq
    HP = N_HEADS * HEAD_PAD
    P = PAGES_PER_STEP
    n_chunks = n_pages // P
    grid_spec = pltpu.PrefetchScalarGridSpec(
        num_scalar_prefetch=1,
        grid=(n_seq, n_chunks),
        in_specs=[
            pl.BlockSpec((1, HQ, KV_LORA), lambda s, c, pt: (s, 0, 0)),
            pl.BlockSpec((1, HQ, QK_ROPE), lambda s, c, pt: (s, 0, 0)),
            pl.BlockSpec((1, nq, KV_ROW), lambda s, c, pt: (s, 0, 0)),
            pl.BlockSpec((N_HEADS * QK_NOPE, KV_LORA), lambda s, c, pt: (0, 0)),
            pl.BlockSpec((KV_LORA, HP), lambda s, c, pt: (0, 0)),
            pl.BlockSpec(memory_space=pl.ANY),
        ],
        out_specs=pl.BlockSpec((1, nq, HP), lambda s, c, pt: (s, 0, 0)),
        scratch_shapes=[
            pltpu.VMEM((2, P, page, KV_ROW), F32),
            pltpu.SemaphoreType.DMA((2, P)),
            pltpu.VMEM((N_HEADS * QK_NOPE + HQ, KV_LORA), BF16),
            pltpu.VMEM((HQ, 1), F32), pltpu.VMEM((HQ, 1), F32), pltpu.VMEM((HQ, KV_LORA), F32),
            pltpu.VMEM((LANES, KV_ROW), F32),
        ],
    )
    return pl.pallas_call(
        functools.partial(_paged_kernel, layer=layer, n_seq=n_seq, n_chunks=n_chunks, page=page),
        grid_spec=grid_spec,
        out_shape=jax.ShapeDtypeStruct((n_seq, nq, HP), F32),
        compiler_params=_cparams(("arbitrary", "arbitrary")),
        name="paged",
    )(page_table.reshape(-1), qabs, qr, new_rows, W["w_ukT"], W["w_uv_p"], cache_kv)


def _expm1(x):
    u = jnp.exp(x)
    safe = jnp.where(u == 1.0, 2.0, jnp.where(u == 0.0, 0.5, u))
    val = (u - 1.0) * x / jnp.log(safe)
    return jnp.where(u == 1.0, x, jnp.where(u == 0.0, -1.0, val))


def _lru_kernel(zxy_ref, buf0_ref, h0_ref, cw_ref, cb_ref, wra_ref, bra_ref, wri_ref, bri_ref,
                lam_ref, o_ref, hl_ref, xbuf, hc, a_s, b_s, *, bb, tt):
    W = LRU_WIDTH
    ti = pl.program_id(1)

    @pl.when(ti == 0)
    def _():
        xbuf[:, 5:8, :] = buf0_ref[...]
        hc[...] = h0_ref[...]

    xbuf[:, 8:8 + tt, :] = zxy_ref[:, :, 0:W]
    xc = cb_ref[...] + cw_ref[0:1, :] * xbuf[:, 5:5 + tt, :]
    for j in range(1, CONV_WIDTH):
        xc = xc + cw_ref[j:j + 1, :] * xbuf[:, 5 + j:5 + j + tt, :]
    xbuf[:, 5:8, :] = xbuf[:, 5 + tt:8 + tt, :]
    xc2 = xc.reshape(bb * tt, W)
    xcb = xc2.astype(BF16)
    r = _sigmoid(jnp.dot(xcb, wra_ref[...], preferred_element_type=F32) + bra_ref[...])
    i = _sigmoid(jnp.dot(xcb, wri_ref[...], preferred_element_type=F32) + bri_ref[...])
    nl = -lam_ref[...]
    softplus = jnp.maximum(nl, 0.0) + jnp.log1p(jnp.exp(-jnp.abs(nl)))
    log_a = -RG_C * r * softplus
    a_s[...] = jnp.exp(log_a).reshape(bb, tt, W)
    b_s[...] = (jnp.sqrt(-_expm1(2.0 * log_a)) * (i * xc2)).reshape(bb, tt, W)

    def body(g, h):
        base = pl.multiple_of(g * SUBLANES, SUBLANES)
        at = a_s[:, pl.ds(base, SUBLANES), :]
        bt = b_s[:, pl.ds(base, SUBLANES), :]
        for k in range(SUBLANES):
            h = at[:, k:k + 1, :] * h + bt[:, k:k + 1, :]
            a_s[:, pl.ds(base + k, 1), :] = h
        return h

    h = lax.fori_loop(0, tt // SUBLANES, body, hc[...])
    hc[...] = h
    hl_ref[...] = h
    o_ref[...] = a_s[...] * _gelu(zxy_ref[:, :, W:2 * W])


def _lru(zxy, buf0, h0, W, bb, tt):
    B, T, _ = zxy.shape
    Wd = LRU_WIDTH
    bspec = lambda t, w: pl.BlockSpec((bb, t, w), lambda b, i: (b, 0, 0))
    return pl.pallas_call(
        functools.partial(_lru_kernel, bb=bb, tt=tt),
        grid=(B // bb, T // tt),
        in_specs=[pl.BlockSpec((bb, tt, 2 * Wd), lambda b, i: (b, i, 0)),
                  bspec(CONV_WIDTH - 1, Wd), bspec(1, Wd),
                  _resident((CONV_WIDTH, Wd)), _resident((1, Wd)),
                  _resident((Wd, Wd)), _resident((1, Wd)), _resident((Wd, Wd)), _resident((1, Wd)),
                  _resident((1, Wd))],
        out_specs=[pl.BlockSpec((bb, tt, Wd), lambda b, i: (b, i, 0)), bspec(1, Wd)],
        out_shape=[jax.ShapeDtypeStruct((B, T, Wd), F32), jax.ShapeDtypeStruct((B, 1, Wd), F32)],
        scratch_shapes=[pltpu.VMEM((bb, tt + 8, Wd), F32), pltpu.VMEM((bb, 1, Wd), F32),
                        pltpu.VMEM((bb, tt, Wd), F32), pltpu.VMEM((bb, tt, Wd), F32)],
        compiler_params=_cparams(("arbitrary", "arbitrary")),
        name="lru",
    )(zxy, buf0, h0, W["conv_w"], W["conv_b"], W["w_ra_bd"], W["b_ra"], W["w_ri_bd"], W["b_ri"],
      W["lam"])


def _gmlp_kernel(zuv_ref, wcat_ref, bias_ref, g_ref, b_ref, o_ref, vn_ref):
    Wd = GM_WIDTH
    u = _gelu(zuv_ref[:, 0:Wd])
    v = _gelu(zuv_ref[:, Wd:2 * Wd])
    vc = v - jnp.mean(v, axis=-1, keepdims=True)
    vn = vc * lax.rsqrt(jnp.mean(vc * vc, axis=-1, keepdims=True) + EPS) * g_ref[...] + b_ref[...]
    vn_ref[...] = vn
    vb = vn.astype(BF16)
    grp = lax.broadcasted_iota(jnp.int32, vb.shape, 1) // GM_GD
    stacked = jnp.concatenate([jnp.where(grp == g, vb, jnp.zeros_like(vb)) for g in range(GM_GROUPS)],
                              axis=0)
    mixed = jnp.dot(wcat_ref[...], stacked, preferred_element_type=F32) + bias_ref[...]
    o_ref[...] = u * mixed


def _gmlp(zuv, wcat, bias, W):
    n = zuv.shape[0]
    Wd = GM_WIDTH
    return pl.pallas_call(
        _gmlp_kernel,
        grid=(n // CHUNK,),
        in_specs=[pl.BlockSpec((CHUNK, 2 * Wd), lambda i: (i, 0)),
                  _resident((CHUNK, GM_GROUPS * CHUNK)), _resident((CHUNK, Wd)),
                  _resident((1, Wd)), _resident((1, Wd))],
        out_specs=[pl.BlockSpec((CHUNK, Wd), lambda i: (i, 0)), pl.BlockSpec((CHUNK, Wd), lambda i: (i, 0))],
        out_shape=[jax.ShapeDtypeStruct((n, Wd), F32), jax.ShapeDtypeStruct((n, Wd), F32)],
        compiler_params=_cparams(("arbitrary",)),
        name="gmlp",
    )(zuv, wcat, bias, W["gm_g"], W["gm_b"])


def _merge_kernel(oa_ref, ob_ref, oc_ref, zg_ref, x_ref, gt_ref, sc_ref, sh_ref, wa_ref, wb_ref,
                  wc_ref, wo_ref, g_ref, x1_ref, h2_ref):
    D = D_MODEL
    merged = _sigmoid(zg_ref[0, :, 0:D]) * jnp.dot(oa_ref[0].astype(BF16), wa_ref[...],
                                                    preferred_element_type=F32)
    merged += _sigmoid(zg_ref[0, :, D:2 * D]) * jnp.dot(ob_ref[0].astype(BF16), wb_ref[...],
                                                         preferred_element_type=F32)
    merged += _sigmoid(zg_ref[0, :, 2 * D:3 * D]) * jnp.dot(oc_ref[0].astype(BF16), wc_ref[...],
                                                             preferred_element_type=F32)
    x1 = x_ref[0] + gt_ref[0] * jnp.dot(merged.astype(BF16), wo_ref[...], preferred_element_type=F32)
    x1_ref[0] = x1
    h2_ref[0] = (_rms(x1, g_ref[...]) * (1.0 + sc_ref[0]) + sh_ref[0]).astype(BF16)


def _merge(oa, ob, oc, zg, x, gt, sc, sh, W, tm):
    G, Tg, D = x.shape
    tok = lambda w: pl.BlockSpec((1, tm, w), lambda g_, i: (g_, i, 0))
    return pl.pallas_call(
        _merge_kernel,
        grid=(G, Tg // tm),
        in_specs=[tok(oa.shape[-1]), tok(LRU_WIDTH), tok(GM_WIDTH), tok(3 * D), tok(D),
                  _mod_spec(gt, tm), _mod_spec(sc, tm), _mod_spec(sh, tm),
                  _resident(W["w_br_a"].shape), _resident((LRU_WIDTH, D)), _resident((GM_WIDTH, D)),
                  _resident((D, D)), _resident((1, D))],
        out_specs=[tok(D), tok(D)],
        out_shape=[jax.ShapeDtypeStruct((G, Tg, D), F32), jax.ShapeDtypeStruct((G, Tg, D), BF16)],
        compiler_params=_cparams(("arbitrary", "arbitrary")),
        name="merge",
    )(oa, ob, oc, zg, x, gt, sc, sh, W["w_br_a"], W["w_br_b"], W["w_br_c"], W["w_out"], W["g_ffn"])


def _bitonic_sort_desc(v):
    n = len(v)
    v = list(v)
    k = 2
    while k <= n:
        j = k // 2
        while j >= 1:
            for i in range(n):
                l = i ^ j
                if l > i:
                    hi, lo = jnp.maximum(v[i], v[l]), jnp.minimum(v[i], v[l])
                    if (i & k) == 0:
                        v[i], v[l] = hi, lo
                    else:
                        v[i], v[l] = lo, hi
            j //= 2
        k *= 2
    return v


def _bitonic_merge_desc(v):
    n = len(v)
    v = list(v)
    j = n // 2
    while j >= 1:
        for i in range(n):
            l = i ^ j
            if l > i:
                v[i], v[l] = jnp.maximum(v[i], v[l]), jnp.minimum(v[i], v[l])
        j //= 2
    return v


def _top16_rows(s):
    v = _bitonic_sort_desc([s[SUBLANES * i:SUBLANES * (i + 1), :] for i in range(N_KEYS // SUBLANES)])
    for shift in (4, 2, 1):
        other = [pltpu.roll(x, shift, 0) for x in v]
        v = _bitonic_merge_desc([jnp.maximum(v[i], other[PEER_TOPK - 1 - i]) for i in range(PEER_TOPK)])
    return v


_STAIR = [(i, j) for i in range(PEER_TOPK) for j in range(PEER_TOPK) if (i + 1) * (j + 1) <= PEER_TOPK]


def _peer_kernel(h2_ref, x_ref, gt_ref, wpq_ref, sk_ref, eu_ref, evT_ref, o_ref,
                 s1_s, s2_s, e1_s, e2_s, tau_s, wT_s, acc_s, *, tm, ablk):
    j = pl.program_id(2)
    nj = pl.num_programs(2)
    H = PEER_HEADS

    @pl.when(j == 0)
    def _():
        q = jnp.dot(h2_ref[0], wpq_ref[...], preferred_element_type=F32).astype(BF16)
        sub = lax.broadcasted_iota(jnp.int32, (SUBLANES, tm), 0)
        T1 = [jnp.zeros((SUBLANES, tm), F32)] * PEER_TOPK
        T2 = [jnp.zeros((SUBLANES, tm), F32)] * PEER_TOPK
        for h in range(H):
            q1 = q[:, h * 2 * PEER_HALF:h * 2 * PEER_HALF + PEER_HALF]
            q2 = q[:, h * 2 * PEER_HALF + PEER_HALF:(h + 1) * 2 * PEER_HALF]
            s1 = lax.dot_general(sk_ref[h, 0], q1, NT_DIMS, preferred_element_type=F32)
            s2 = lax.dot_general(sk_ref[h, 1], q2, NT_DIMS, preferred_element_type=F32)
            s1_s[h] = s1
            s2_s[h] = s2
            t1 = _top16_rows(s1)
            t2 = _top16_rows(s2)
            T1 = [jnp.where(sub == h, t1[i], T1[i]) for i in range(PEER_TOPK)]
            T2 = [jnp.where(sub == h, t2[i], T2[i]) for i in range(PEER_TOPK)]
        cands = [T1[a] + T2[b] for (a, b) in _STAIR]
        m = T1[0] + T2[0]
        z = jnp.zeros((SUBLANES, tm), F32)
        top = m
        for it in range(PEER_TOPK):
            top = functools.reduce(jnp.maximum, cands)
            z = z + jnp.exp(top - m)
            if it + 1 < PEER_TOPK:
                cands = [jnp.where(c == top, NEG, c) for c in cands]
        tau_s[...] = top
        rz = 1.0 / z
        for h in range(H):
            e1_s[h] = jnp.exp(s1_s[h] - T1[0][h:h + 1, :]) * rz[h:h + 1, :]
            e2_s[h] = jnp.exp(s2_s[h] - T2[0][h:h + 1, :])
        acc_s[...] = jnp.zeros(acc_s.shape, F32)

    act = lax.dot_general(eu_ref[...], h2_ref[0], NT_DIMS, preferred_element_type=F32)
    base = pl.multiple_of(j * ablk, ablk)
    for i in range(ablk):
        g = jnp.zeros((N_KEYS, tm), F32)
        for h in range(H):
            s1a = s1_s[h, pl.ds(base + i, 1), :]
            e1a = e1_s[h, pl.ds(base + i, 1), :]
            keep = (s1a + s2_s[h]) >= tau_s[h:h + 1, :]
            g = g + jnp.where(keep, e2_s[h], 0.0) * e1a
        a = act[i * N_KEYS:(i + 1) * N_KEYS, :]
        wT_s[i * N_KEYS:(i + 1) * N_KEYS, :] = (g * _gelu(a)).astype(BF16)
    acc_s[...] += jnp.dot(evT_ref[...], wT_s[...], preferred_element_type=F32)

    @pl.when(j == nj - 1)
    def _():
        o_ref[0] = x_ref[0] + gt_ref[0] * acc_s[...].T


def _peer(h2, x1, gt, W, tm, ablk=8):
    G, Tg, D = x1.shape
    n_exp = N_KEYS * N_KEYS
    eb = ablk * N_KEYS
    tok = lambda w: pl.BlockSpec((1, tm, w), lambda g_, i, j: (g_, i, 0))
    if gt.shape[1] == 1:
        gt_spec = pl.BlockSpec((1, 1, D), lambda g_, i, j: (g_, 0, 0))
    else:
        gt_spec = tok(D)
    H = PEER_HEADS
    return pl.pallas_call(
        functools.partial(_peer_kernel, tm=tm, ablk=ablk),
        grid=(G, Tg // tm, n_exp // eb),
        in_specs=[tok(D), tok(D), gt_spec,
                  _resident((D, H * 2 * PEER_HALF)), _resident((H, 2, N_KEYS, PEER_HALF)),
                  pl.BlockSpec((eb, D), lambda g_, i, j: (j, 0)),
                  pl.BlockSpec((D, eb), lambda g_, i, j: (0, j))],
        out_specs=tok(D),
        out_shape=jax.ShapeDtypeStruct((G, Tg, D), F32),
        scratch_shapes=[pltpu.VMEM((H, N_KEYS, tm), F32), pltpu.VMEM((H, N_KEYS, tm), F32),
                        pltpu.VMEM((H, N_KEYS, tm), F32), pltpu.VMEM((H, N_KEYS, tm), F32),
                        pltpu.VMEM((SUBLANES, tm), F32), pltpu.VMEM((eb, tm), BF16),
                        pltpu.VMEM((D, tm), F32)],
        compiler_params=_cparams(("arbitrary", "arbitrary", "arbitrary")),
        name="peer",
    )(h2, x1, gt, W["w_pq"], W["sub_keys"], W["eu"], W["evT"])


def _pad_heads(w, per_head):
    lead = w.shape[:-1]
    w = w.reshape(lead + (N_HEADS, per_head))
    w = jnp.pad(w, [(0, 0)] * len(lead) + [(0, 0), (0, HEAD_PAD - per_head)])
    return w.reshape(lead + (N_HEADS * HEAD_PAD,))


def _block_diag(w):
    H, d, _ = w.shape
    eye = jnp.eye(H, dtype=w.dtype)
    return (eye[:, None, :, None] * w[:, :, None, :]).reshape(H * d, H * d)


def _prep_layer(l, w_in, g_norm_mix, g_norm_ffn, g_q_lora, w_uq, g_kv_lora, w_uk, w_uv, g_qn, g_kn,
                conv_w, conv_b, w_ra, b_ra, w_ri, b_ri, lru_lambda, gm_ln_g, gm_ln_b, w_s, b_s,
                w_branch, w_out, w_pq, sub_keys, expert_u, expert_v):
    D = D_MODEL
    wi = w_in[l]
    c0 = Q_LORA + KV_LORA
    zc = lambda n: jnp.zeros((D, n), F32)
    w_in_p = jnp.concatenate([wi[:, :c0], zc(QK_NOPE), wi[:, c0:c0 + QK_ROPE], zc(HEAD_PAD - QK_DIM),
                              wi[:, c0 + QK_ROPE:]], axis=1).astype(BF16)
    w_uk_pf = _pad_heads(w_uk[l], QK_NOPE)
    pad1 = lambda g: jnp.pad(g, (0, HEAD_PAD - g.shape[0])).reshape(1, HEAD_PAD)
    W = dict(
        w_in_p=w_in_p,
        g_mix=g_norm_mix[l].reshape(1, D), g_ffn=g_norm_ffn[l].reshape(1, D),
        g_q=g_q_lora[l].reshape(1, Q_LORA), g_kv=g_kv_lora[l].reshape(1, KV_LORA),
        w_uq_p=_pad_heads(w_uq[l], QK_DIM).astype(BF16),
        w_uk_p=w_uk_pf.astype(BF16), w_uk_pf=w_uk_pf,
        w_ukT=w_uk[l].T.astype(BF16),
        w_uv_p=_pad_heads(w_uv[l], V_DIM).astype(BF16),
        g_qn_p=pad1(g_qn[l]), g_kn_p=pad1(g_kn[l]),
        conv_w=conv_w[l], conv_b=conv_b[l].reshape(1, -1),
        w_ra_bd=_block_diag(w_ra[l]).astype(BF16), b_ra=b_ra[l].reshape(1, -1),
        w_ri_bd=_block_diag(w_ri[l]).astype(BF16), b_ri=b_ri[l].reshape(1, -1),
        lam=lru_lambda[l].reshape(1, -1),
        gm_g=gm_ln_g[l].reshape(1, -1), gm_b=gm_ln_b[l].reshape(1, -1),
        w_br_a=_pad_heads(w_branch[l, 0].T, V_DIM).T.astype(BF16),
        w_br_b=w_branch[l, 1].astype(BF16), w_br_c=w_branch[l, 2].astype(BF16),
        w_out=w_out[l].astype(BF16),
        w_pq=w_pq[l].astype(BF16), sub_keys=sub_keys[l].astype(BF16),
        eu=expert_u[l].astype(BF16), evT=expert_v[l].T.astype(BF16),
    )
    ws = w_s[l]
    bs = b_s[l]
    tril = jnp.tril(ws)
    W["gm_wcat_p"] = tril.transpose(1, 0, 2).reshape(CHUNK, GM_GROUPS * CHUNK).astype(BF16)
    W["gm_bias_p"] = jnp.repeat(bs.T, GM_GD, axis=1)
    return W, ws, bs


def _gmlp_sample_mats(ws, bs, ts):
    reps = CHUNK // ts
    small = jnp.tril(ws[:, :ts, :ts])
    eye = jnp.eye(reps, dtype=F32)
    big = (eye[None, :, None, :, None] * small[:, None, :, None, :]).reshape(GM_GROUPS, CHUNK, CHUNK)
    wcat = big.transpose(1, 0, 2).reshape(CHUNK, GM_GROUPS * CHUNK).astype(BF16)
    bias = jnp.tile(jnp.repeat(bs[:, :ts].T, GM_GD, axis=1), (reps, 1))
    return wcat, bias


def _rope_tables(pos):
    half = QK_ROPE // 2
    freq = ROPE_THETA ** (-jnp.arange(half, dtype=F32) / half)
    ang = pos.astype(F32)[:, None] * freq[None, :]
    cos, sin = jnp.cos(ang), jnp.sin(ang)
    n = pos.shape[0]
    z = lambda w: jnp.zeros((n, w), F32)
    c = jnp.concatenate([jnp.ones((n, QK_NOPE), F32), cos, cos, z(HEAD_PAD - QK_DIM)], axis=1)
    sa = jnp.concatenate([z(QK_NOPE + half), sin, z(HEAD_PAD - QK_DIM)], axis=1)
    sb = jnp.concatenate([z(QK_NOPE), -sin, z(half + HEAD_PAD - QK_DIM)], axis=1)
    return c, sa, sb


def _split_mod(mod):
    return [mod[..., k * D_MODEL:(k + 1) * D_MODEL] for k in range(6)]


def _layer_group(W, x, mods, tabs, attend, buf0, h0, gm_mats, tm, lru_bt, q_dtype):
    G, Tg, D = x.shape
    sh1, sc1, gt1, sh2, sc2, gt2 = mods
    zq, zkvr, zxy, zuv, zg = _inproj(x, sc1, sh1, W["g_mix"], W["w_in_p"], tm)
    q, k, v, rows = _qkprep(zq, zkvr, tabs, W, tm, q_dtype)
    o_a = attend(q, k, v, rows)
    B, T = buf0.shape[0], (G * Tg) // buf0.shape[0]
    zxy_b = zxy.reshape(B, T, 2 * LRU_WIDTH)
    bb, tt = lru_bt
    o_b, h_last = _lru(zxy_b, buf0, h0, W, bb, tt)
    conv_new = zxy_b[:, T - (CONV_WIDTH - 1):, :LRU_WIDTH]
    o_c, vn = _gmlp(zuv.reshape(G * Tg, 2 * GM_WIDTH), gm_mats[0], gm_mats[1], W)
    x1, h2 = _merge(o_a, o_b.reshape(G, Tg, -1), o_c.reshape(G, Tg, -1), zg, x, gt1, sc2, sh2, W, tm)
    xo = _peer(h2, x1, gt2, W, tm)
    return xo, rows, h_last.reshape(B, LRU_WIDTH), conv_new, vn


def kernel(x_prompt, x_sample, c_prompt, c_sample, cache_kv, state_lru_h, state_conv, page_table, w_ada, b_ada, g_norm_mix, g_norm_ffn, w_in, g_q_lora, w_uq, g_kv_lora, w_uk, w_uv, g_qn, g_kn, conv_w, conv_b, w_ra, b_ra, w_ri, b_ri, lru_lambda, gm_ln_g, gm_ln_b, w_s, b_s, w_branch, w_out, w_pq, sub_keys, expert_u, expert_v):
    Bp, Tp, D = x_prompt.shape
    Bs, Ts, _ = x_sample.shape
    n_past = page_table.shape[1] * cache_kv.shape[2]
    TM = 256

    c_all = jnp.concatenate([c_prompt, c_sample], axis=0)
    nb = c_all.shape[0]
    nb_pad = -(-nb // SUBLANES) * SUBLANES
    c_all = jnp.pad(c_all, ((0, nb_pad - nb), (0, 0)))
    mod_all = _ada(c_all, w_ada, b_ada)

    tabs_p = _rope_tables(jnp.arange(Tp, dtype=F32))
    tabs_s = tuple(jnp.tile(t, (Bs, 1)) for t in _rope_tables(n_past + jnp.arange(Ts, dtype=F32)))

    xp = x_prompt
    xs = x_sample.reshape(1, Bs * Ts, D)
    buf0_p = jnp.zeros((Bp, CONV_WIDTH - 1, LRU_WIDTH), F32)
    h0_p = jnp.zeros((Bp, 1, LRU_WIDTH), F32)
    outs = [[] for _ in range(8)]
    for l in range(DEPTH):
        W, ws, bs = _prep_layer(l, w_in, g_norm_mix, g_norm_ffn, g_q_lora, w_uq, g_kv_lora, w_uk, w_uv,
                                g_qn, g_kn, conv_w, conv_b, w_ra, b_ra, w_ri, b_ri, lru_lambda, gm_ln_g,
                                gm_ln_b, w_s, b_s, w_branch, w_out, w_pq, sub_keys, expert_u, expert_v)
        mods_p = [m[:, None, :] for m in _split_mod(mod_all[l, :Bp])]
        mods_s = [jnp.repeat(m, Ts, axis=0)[None] for m in _split_mod(mod_all[l, Bp:Bp + Bs])]

        def attend_p(q, k, v, rows):
            return _flash(q, k, v)

        def attend_s(q, k, v, rows, W=W, l=l):
            qabs, qg = _qabs(q.reshape(Bs * Ts, -1), W)
            qabs = qabs.reshape(Bs, Ts, N_HEADS, KV_LORA).transpose(0, 2, 1, 3)
            qabs = qabs.reshape(Bs, N_HEADS * Ts, KV_LORA).astype(BF16)
            qr = qg.reshape(Bs, Ts, N_HEADS, HEAD_PAD)[..., QK_NOPE:QK_DIM].transpose(0, 2, 1, 3)
            qr = qr.reshape(Bs, N_HEADS * Ts, QK_ROPE).astype(BF16)
            o = _paged(page_table, qabs, qr, rows.reshape(Bs, Ts, KV_ROW), cache_kv, W, l)
            return o.reshape(1, Bs * Ts, -1)

        xp, rp, hp, cp, vp = _layer_group(W, xp, mods_p, tabs_p, attend_p, buf0_p, h0_p,
                                          (W["gm_wcat_p"], W["gm_bias_p"]), TM, (Bp, 256), BF16)
        xs, rs, hs, cs, vs = _layer_group(W, xs, mods_s, tabs_s, attend_s, state_conv[:, l],
                                          state_lru_h[:, l][:, None, :], _gmlp_sample_mats(ws, bs, Ts),
                                          TM, (16, Ts), F32)
        outs[0].append(rp)
        outs[1].append(rs.reshape(Bs, Ts, KV_ROW))
        outs[2].append(hp)
        outs[3].append(hs)
        outs[4].append(cp)
        outs[5].append(cs)
        outs[6].append(vp.reshape(Bp, Tp, GM_WIDTH)[:, -CHUNK:])
        outs[7].append(vs.reshape(Bs, Ts, GM_WIDTH))
    stacked = [jnp.stack(o, axis=1) for o in outs]
    return (xp, xs.reshape(Bs, Ts, D), *stacked)
```

```python
import functools
import math

import numpy as np
import jax
import jax.numpy as jnp
from jax import lax
from jax.experimental import pallas as pl
from jax.experimental.pallas import tpu as pltpu

F32 = jnp.float32
BF16 = jnp.bfloat16

D_MODEL = 1024
DEPTH = 2
N_HEADS = 8
Q_LORA = 384
KV_LORA = 256
QK_NOPE = 64
QK_ROPE = 32
QK_DIM = QK_NOPE + QK_ROPE
V_DIM = 64
KV_ROW = KV_LORA + QK_ROPE
ROPE_THETA = 10000.0
LRU_WIDTH = 512
LRU_HEADS = 8
LRU_HD = LRU_WIDTH // LRU_HEADS
CONV_WIDTH = 4
RG_C = 8.0
GM_WIDTH = 512
GM_GROUPS = 8
GM_GD = GM_WIDTH // GM_GROUPS
CHUNK = 128
PEER_HEADS = 8
N_KEYS = 128
PEER_HALF = 128
PEER_TOPK = 16
EPS = 1e-6
NEG = -1e30

LANES = 128
SUBLANES = 8
HEAD_PAD = LANES
VMEM_LIMIT = 56 * 1024 * 1024

NT_DIMS = (((1,), (1,)), ((), ()))


def _cparams(sem):
    return pltpu.CompilerParams(dimension_semantics=sem, vmem_limit_bytes=VMEM_LIMIT)


def _resident(shape):
    nd = len(shape)
    return pl.BlockSpec(shape, lambda *_: (0,) * nd, pipeline_mode=pl.Buffered(1))


def _gelu(x):
    return 0.5 * x * (1.0 + lax.erf(x * (2.0 ** -0.5)))


def _sigmoid(x):
    return jax.nn.sigmoid(x)


def _rms(x, g, n=None):
    n = x.shape[-1] if n is None else n
    ss = jnp.sum(x * x, axis=-1, keepdims=True)
    return x * lax.rsqrt(ss * (1.0 / n) + EPS) * g


def _ada_kernel(c_ref, w_ref, b_ref, o_ref):
    c = c_ref[...]
    s = (c * _sigmoid(c)).astype(BF16)
    o_ref[0] = jnp.dot(s, w_ref[0].astype(BF16), preferred_element_type=F32) + b_ref[0]


def _ada(c_all, w_ada, b_ada):
    nb = c_all.shape[0]
    tn = 512
    ncol = w_ada.shape[-1]
    return pl.pallas_call(
        _ada_kernel,
        grid=(DEPTH, ncol // tn),
        in_specs=[
            pl.BlockSpec((nb, D_MODEL), lambda l, j: (0, 0)),
            pl.BlockSpec((1, D_MODEL, tn), lambda l, j: (l, 0, j)),
            pl.BlockSpec((1, 1, tn), lambda l, j: (l, 0, j)),
        ],
        out_specs=pl.BlockSpec((1, nb, tn), lambda l, j: (l, 0, j)),
        out_shape=jax.ShapeDtypeStruct((DEPTH, nb, ncol), F32),
        compiler_params=_cparams(("arbitrary", "arbitrary")),
        name="ada",
    )(c_all, w_ada, b_ada.reshape(DEPTH, 1, ncol))


def _mod_spec(mod, tm):
    if mod.shape[1] == 1:
        return pl.BlockSpec((1, 1, mod.shape[2]), lambda g, i: (g, 0, 0))
    return pl.BlockSpec((1, tm, mod.shape[2]), lambda g, i: (g, i, 0))


Z_SPLITS = (Q_LORA, 384, 2 * LRU_WIDTH, 2 * GM_WIDTH, 3 * D_MODEL)
Z_COLS = sum(Z_SPLITS)


def _inproj_kernel(x_ref, sc_ref, sh_ref, g_ref, w_ref, *o_refs):
    x = x_ref[0]
    h = _rms(x, g_ref[...]) * (1.0 + sc_ref[0]) + sh_ref[0]
    hb = h.astype(BF16)
    off = 0
    for o_ref, width in zip(o_refs, Z_SPLITS):
        o_ref[0] = jnp.dot(hb, w_ref[:, off:off + width], preferred_element_type=F32)
        off += width


def _inproj(x, sc, sh, g, w_in_p, tm):
    G, Tg, D = x.shape
    out_shape = [jax.ShapeDtypeStruct((G, Tg, w), F32) for w in Z_SPLITS]
    out_specs = [pl.BlockSpec((1, tm, w), lambda g_, i: (g_, i, 0)) for w in Z_SPLITS]
    return pl.pallas_call(
        _inproj_kernel,
        grid=(G, Tg // tm),
        in_specs=[
            pl.BlockSpec((1, tm, D), lambda g_, i: (g_, i, 0)),
            _mod_spec(sc, tm), _mod_spec(sh, tm),
            _resident((1, D)), _resident((D, Z_COLS)),
        ],
        out_specs=out_specs,
        out_shape=out_shape,
        compiler_params=_cparams(("arbitrary", "arbitrary")),
        name="inproj",
    )(x, sc, sh, g, w_in_p)


def _rope_block(blk, c, sa, sb):
    n = blk.shape[-1]
    return blk * c + pltpu.roll(blk, 16, 1) * sa + pltpu.roll(blk, n - 16, 1) * sb


def _qkprep_kernel(zq_ref, zkvr_ref, c_ref, sa_ref, sb_ref, gq_ref, wuq_ref, gkv_ref, wuk_ref,
                   wuv_ref, gqn_ref, gkn_ref, q_ref, k_ref, v_ref, rows_ref):
    c, sa, sb = c_ref[...], sa_ref[...], sb_ref[...]
    qn = _rms(zq_ref[0], gq_ref[...]).astype(BF16)
    qf = jnp.dot(qn, wuq_ref[...], preferred_element_type=F32)
    gqn, gkn = gqn_ref[...], gkn_ref[...]
    for h in range(N_HEADS):
        sl = slice(h * HEAD_PAD, (h + 1) * HEAD_PAD)
        r = _rope_block(qf[:, sl], c, sa, sb)
        q_ref[0, :, sl] = _rms(r, gqn, QK_DIM).astype(q_ref.dtype)
    zkvr = zkvr_ref[0]
    lat = _rms(zkvr[:, :KV_LORA], gkv_ref[...])
    krb = _rope_block(zkvr[:, KV_LORA:], c, sa, sb)
    rows_ref[0, :, 0:KV_LORA] = lat
    rows_ref[0, :, KV_LORA:KV_ROW] = krb[:, QK_NOPE:QK_DIM]
    latb = lat.astype(BF16)
    knp = jnp.dot(latb, wuk_ref[...], preferred_element_type=F32)
    for h in range(N_HEADS):
        sl = slice(h * HEAD_PAD, (h + 1) * HEAD_PAD)
        k_ref[0, :, sl] = _rms(knp[:, sl] + krb, gkn, QK_DIM).astype(BF16)
    v = jnp.dot(latb, wuv_ref[...], preferred_element_type=F32)
    one_lane = lax.broadcasted_iota(jnp.int32, v.shape, 1) % HEAD_PAD == V_DIM
    v_ref[0] = jnp.where(one_lane, 1.0, v).astype(BF16)


def _qkprep(zq, zkvr, tabs, W, tm, q_dtype):
    G, Tg, _ = zq.shape
    c, sa, sb = tabs
    HP = N_HEADS * HEAD_PAD
    tok = lambda w: pl.BlockSpec((1, tm, w), lambda g_, i: (g_, i, 0))
    tab = pl.BlockSpec((tm, HEAD_PAD), lambda g_, i: (i, 0))
    return pl.pallas_call(
        _qkprep_kernel,
        grid=(G, Tg // tm),
        in_specs=[tok(Q_LORA), tok(384), tab, tab, tab,
                  _resident((1, Q_LORA)), _resident((Q_LORA, HP)), _resident((1, KV_LORA)),
                  _resident((KV_LORA, HP)), _resident((KV_LORA, HP)),
                  _resident((1, HEAD_PAD)), _resident((1, HEAD_PAD))],
        out_specs=[tok(HP), tok(HP), tok(HP), tok(KV_ROW)],
        out_shape=[jax.ShapeDtypeStruct((G, Tg, HP), q_dtype),
                   jax.ShapeDtypeStruct((G, Tg, HP), BF16),
                   jax.ShapeDtypeStruct((G, Tg, HP), BF16),
                   jax.ShapeDtypeStruct((G, Tg, KV_ROW), F32)],
        compiler_params=_cparams(("arbitrary", "arbitrary")),
        name="qkprep",
    )(zq, zkvr, c, sa, sb, W["g_q"], W["w_uq_p"], W["g_kv"], W["w_uk_p"], W["w_uv_p"],
      W["g_qn_p"], W["g_kn_p"])


def _flash_kernel(q_ref, k_ref, v_ref, o_ref, m_ref, acc_ref, *, tq, tk):
    qi, ki = pl.program_id(1), pl.program_id(2)
    c = (QK_DIM ** -0.5) * math.log2(math.e)

    @pl.when(ki == 0)
    def _():
        m_ref[...] = jnp.full(m_ref.shape, NEG, F32)
        acc_ref[...] = jnp.zeros(acc_ref.shape, F32)

    def step(diagonal):
        if diagonal:
            causal = (lax.broadcasted_iota(jnp.int32, (tq, tk), 0)
                      >= lax.broadcasted_iota(jnp.int32, (tq, tk), 1))
        for h in range(N_HEADS):
            sl = slice(h * HEAD_PAD, (h + 1) * HEAD_PAD)
            s = lax.dot_general(q_ref[0, :, sl], k_ref[0, :, sl], NT_DIMS, preferred_element_type=F32)
            if diagonal:
                s = jnp.where(causal, s, NEG)
            m_prev = m_ref[h]
            m_new = jnp.maximum(m_prev, jnp.max(s, axis=-1, keepdims=True))
            alpha = jnp.exp2((m_prev - m_new) * c)
            p = jnp.exp2((s - m_new) * c)
            acc_ref[h] = alpha * acc_ref[h] + jnp.dot(p.astype(BF16), v_ref[0, :, sl],
                                                      preferred_element_type=F32)
            m_ref[h] = m_new

    @pl.when(ki < qi)
    def _():
        step(False)

    @pl.when(ki == qi)
    def _():
        step(True)
        for h in range(N_HEADS):
            sl = slice(h * HEAD_PAD, (h + 1) * HEAD_PAD)
            acc = acc_ref[h]
            o_ref[0, :, sl] = (acc / acc[:, V_DIM:V_DIM + 1]).astype(o_ref.dtype)


def _flash(q, k, v, tq=512):
    B, T, HP = q.shape
    tk = tq
    kv_spec = pl.BlockSpec((1, tk, HP), lambda b, i, j: (b, jnp.minimum(i, j), 0))
    return pl.pallas_call(
        functools.partial(_flash_kernel, tq=tq, tk=tk),
        grid=(B, T // tq, T // tk),
        in_specs=[pl.BlockSpec((1, tq, HP), lambda b, i, j: (b, i, 0)), kv_spec, kv_spec],
        out_specs=pl.BlockSpec((1, tq, HP), lambda b, i, j: (b, i, 0)),
        out_shape=jax.ShapeDtypeStruct((B, T, HP), BF16),
        scratch_shapes=[pltpu.VMEM((N_HEADS, tq, 1), F32), pltpu.VMEM((N_HEADS, tq, HEAD_PAD), F32)],
        compiler_params=_cparams(("arbitrary", "arbitrary", "arbitrary")),
        name="flash",
    )(q, k, v)


def _qabs_kernel(q_ref, gkn_ref, wuk_ref, qabs_ref, qg_ref):
    g = gkn_ref[...]
    for h in range(N_HEADS):
        sl = slice(h * HEAD_PAD, (h + 1) * HEAD_PAD)
        qg = q_ref[:, sl] * g
        qg_ref[:, sl] = qg
        qabs_ref[:, h * KV_LORA:(h + 1) * KV_LORA] = lax.dot_general(
            qg, wuk_ref[:, sl], NT_DIMS, preferred_element_type=F32,
            precision=lax.Precision.HIGHEST)


def _qabs(q2d, W):
    n, HP = q2d.shape
    tm = 256
    return pl.pallas_call(
        _qabs_kernel,
        grid=(n // tm,),
        in_specs=[pl.BlockSpec((tm, HP), lambda i: (i, 0)), _resident((1, HEAD_PAD)),
                  _resident((KV_LORA, HP))],
        out_specs=[pl.BlockSpec((tm, N_HEADS * KV_LORA), lambda i: (i, 0)),
                   pl.BlockSpec((tm, HP), lambda i: (i, 0))],
        out_shape=[jax.ShapeDtypeStruct((n, N_HEADS * KV_LORA), F32),
                   jax.ShapeDtypeStruct((n, HP), F32)],
        compiler_params=_cparams(("arbitrary",)),
        name="qabs",
    )(q2d, W["g_kn_p"], W["w_uk_pf"])


PAGES_PER_STEP = 16


def _paged_kernel(pt_ref, qabs_ref, qr_ref, newT_ref, wukT_ref, wuv_ref, cache_ref, o_ref,
                  buf, sem, lhs, m_ref, l_ref, acc_ref, newbuf, *, layer, n_seq, n_chunks, page):
    P = PAGES_PER_STEP
    s_id, c_id = pl.program_id(0), pl.program_id(1)
    step = s_id * n_chunks + c_id
    total = n_seq * n_chunks
    nq = newT_ref.shape[2]
    HQ = N_HEADS * nq
    scale = QK_DIM ** -0.5

    def page_copy(slot, p, page_id):
        return pltpu.make_async_copy(cache_ref.at[page_id, layer], buf.at[slot, p], sem.at[slot, p])

    def start_fetch(step_, slot):
        for p in range(P):
            page_copy(slot, p, pt_ref[step_ * P + p]).start()

    def wait_fetch(slot):
        for p in range(P):
            page_copy(slot, p, 0).wait()

    @pl.when(step == 0)
    def _():
        lhs[0:N_HEADS * QK_NOPE, :] = wukT_ref[...]
        start_fetch(0, 0)

    @pl.when(step + 1 < total)
    def _():
        start_fetch(step + 1, (step + 1) % 2)

    @pl.when(c_id == 0)
    def _():
        lhs[N_HEADS * QK_NOPE:, :] = qabs_ref[0]
        m_ref[...] = jnp.full(m_ref.shape, NEG, F32)
        l_ref[...] = jnp.zeros(l_ref.shape, F32)
        acc_ref[...] = jnp.zeros(acc_ref.shape, F32)

    def process(latT, krT, mask):
        big = jnp.dot(lhs[...], latT, preferred_element_type=F32)
        sc = big[N_HEADS * QK_NOPE:] + jnp.dot(qr_ref[0], krT.astype(BF16), preferred_element_type=F32)
        ssk = jnp.sum(krT * krT, axis=0, keepdims=True)
        parts = []
        for h in range(N_HEADS):
            kn = big[h * QK_NOPE:(h + 1) * QK_NOPE]
            ss = jnp.sum(kn * kn, axis=0, keepdims=True) + ssk
            rs = lax.rsqrt(ss * (1.0 / QK_DIM) + EPS)
            parts.append(sc[h * nq:(h + 1) * nq] * rs)
        s = jnp.concatenate(parts, axis=0) * scale
        if mask is not None:
            s = jnp.where(mask, s, NEG)
        m_prev = m_ref[...]
        m_new = jnp.maximum(m_prev, jnp.max(s, axis=-1, keepdims=True))
        alpha = jnp.exp(m_prev - m_new)
        p = jnp.exp(s - m_new)
        l_ref[...] = alpha * l_ref[...] + jnp.sum(p, axis=-1, keepdims=True)
        acc_ref[...] = alpha * acc_ref[...] + lax.dot_general(p.astype(BF16), latT, NT_DIMS,
                                                              preferred_element_type=F32)
        m_ref[...] = m_new

    slot = step % 2
    wait_fetch(slot)
    latT = jnp.concatenate([buf[slot, p, 0:KV_LORA, :].astype(BF16) for p in range(P)], axis=1)
    krT = jnp.concatenate([buf[slot, p, KV_LORA:KV_ROW, :] for p in range(P)], axis=1)
    process(latT, krT, None)

    @pl.when(c_id == n_chunks - 1)
    def _():
        newbuf[...] = jnp.zeros(newbuf.shape, F32)
        newbuf[:, 0:nq] = newT_ref[0]
        key = lax.broadcasted_iota(jnp.int32, (HQ, newbuf.shape[1]), 1)
        qpos = lax.broadcasted_iota(jnp.int32, (HQ, newbuf.shape[1]), 0) % nq
        process(newbuf[0:KV_LORA, :].astype(BF16), newbuf[KV_LORA:KV_ROW, :], key <= qpos)
        for h in range(N_HEADS):
            ctx = (acc_ref[h * nq:(h + 1) * nq] / l_ref[h * nq:(h + 1) * nq]).astype(BF16)
            sl = slice(h * HEAD_PAD, (h + 1) * HEAD_PAD)
            o_ref[0, :, sl] = jnp.dot(ctx, wuv_ref[:, sl], preferred_element_type=F32)


def _paged(page_table, qabs, qr, new_rows, cache_t, W, layer):
    n_seq, n_pages = page_table.shape
    page = cache_t.shape[3]
    nq = new_rows.shape[1]
    new_t = jnp.swapaxes(new_rows, 1, 2)
    HQ = N_HEADS * nq
    HP = N_HEADS * HEAD_PAD
    P = PAGES_PER_STEP
    n_chunks = n_pages // P
    grid_spec = pltpu.PrefetchScalarGridSpec(
        num_scalar_prefetch=1,
        grid=(n_seq, n_chunks),
        in_specs=[
            pl.BlockSpec((1, HQ, KV_LORA), lambda s, c, pt: (s, 0, 0)),
            pl.BlockSpec((1, HQ, QK_ROPE), lambda s, c, pt: (s, 0, 0)),
            pl.BlockSpec((1, KV_ROW, nq), lambda s, c, pt: (s, 0, 0)),
            pl.BlockSpec((N_HEADS * QK_NOPE, KV_LORA), lambda s, c, pt: (0, 0)),
            pl.BlockSpec((KV_LORA, HP), lambda s, c, pt: (0, 0)),
            pl.BlockSpec(memory_space=pl.ANY),
        ],
        out_specs=pl.BlockSpec((1, nq, HP), lambda s, c, pt: (s, 0, 0)),
        scratch_shapes=[
            pltpu.VMEM((2, P, KV_ROW, page), F32),
            pltpu.SemaphoreType.DMA((2, P)),
            pltpu.VMEM((N_HEADS * QK_NOPE + HQ, KV_LORA), BF16),
            pltpu.VMEM((HQ, 1), F32), pltpu.VMEM((HQ, 1), F32), pltpu.VMEM((HQ, KV_LORA), F32),
            pltpu.VMEM((KV_ROW, LANES), F32),
        ],
    )
    return pl.pallas_call(
        functools.partial(_paged_kernel, layer=layer, n_seq=n_seq, n_chunks=n_chunks, page=page),
        grid_spec=grid_spec,
        out_shape=jax.ShapeDtypeStruct((n_seq, nq, HP), F32),
        compiler_params=_cparams(("arbitrary", "arbitrary")),
        name="paged",
    )(page_table.reshape(-1), qabs, qr, new_t, W["w_ukT"], W["w_uv_p"], cache_t)


def _expm1(x):
    u = jnp.exp(x)
    safe = jnp.where(u == 1.0, 2.0, jnp.where(u == 0.0, 0.5, u))
    val = (u - 1.0) * x / jnp.log(safe)
    return jnp.where(u == 1.0, x, jnp.where(u == 0.0, -1.0, val))


def _lru_kernel(zxy_ref, buf0_ref, h0_ref, cw_ref, cb_ref, wra_ref, bra_ref, wri_ref, bri_ref,
                lam_ref, o_ref, hl_ref, xbuf, hc, a_s, b_s, *, bb, tt):
    W = LRU_WIDTH
    ti = pl.program_id(1)

    @pl.when(ti == 0)
    def _():
        xbuf[:, 5:8, :] = buf0_ref[...]
        hc[...] = h0_ref[...]

    xbuf[:, 8:8 + tt, :] = zxy_ref[:, :, 0:W]
    xc = cb_ref[...] + cw_ref[0:1, :] * xbuf[:, 5:5 + tt, :]
    for j in range(1, CONV_WIDTH):
        xc = xc + cw_ref[j:j + 1, :] * xbuf[:, 5 + j:5 + j + tt, :]
    xbuf[:, 5:8, :] = xbuf[:, 5 + tt:8 + tt, :]
    xc2 = xc.reshape(bb * tt, W)
    xcb = xc2.astype(BF16)
    r = _sigmoid(jnp.dot(xcb, wra_ref[...], preferred_element_type=F32) + bra_ref[...])
    i = _sigmoid(jnp.dot(xcb, wri_ref[...], preferred_element_type=F32) + bri_ref[...])
    nl = -lam_ref[...]
    softplus = jnp.maximum(nl, 0.0) + jnp.log1p(jnp.exp(-jnp.abs(nl)))
    log_a = -RG_C * r * softplus
    a_s[...] = jnp.exp(log_a).reshape(bb, tt, W)
    b_s[...] = (jnp.sqrt(-_expm1(2.0 * log_a)) * (i * xc2)).reshape(bb, tt, W)

    def body(g, h):
        base = pl.multiple_of(g * SUBLANES, SUBLANES)
        at = a_s[:, pl.ds(base, SUBLANES), :]
        bt = b_s[:, pl.ds(base, SUBLANES), :]
        for k in range(SUBLANES):
            h = at[:, k:k + 1, :] * h + bt[:, k:k + 1, :]
            a_s[:, pl.ds(base + k, 1), :] = h
        return h

    h = lax.fori_loop(0, tt // SUBLANES, body, hc[...])
    hc[...] = h
    hl_ref[...] = h
    o_ref[...] = a_s[...] * _gelu(zxy_ref[:, :, W:2 * W])


def _lru(zxy, buf0, h0, W, bb, tt):
    B, T, _ = zxy.shape
    Wd = LRU_WIDTH
    bspec = lambda t, w: pl.BlockSpec((bb, t, w), lambda b, i: (b, 0, 0))
    return pl.pallas_call(
        functools.partial(_lru_kernel, bb=bb, tt=tt),
        grid=(B // bb, T // tt),
        in_specs=[pl.BlockSpec((bb, tt, 2 * Wd), lambda b, i: (b, i, 0)),
                  bspec(CONV_WIDTH - 1, Wd), bspec(1, Wd),
                  _resident((CONV_WIDTH, Wd)), _resident((1, Wd)),
                  _resident((Wd, Wd)), _resident((1, Wd)), _resident((Wd, Wd)), _resident((1, Wd)),
                  _resident((1, Wd))],
        out_specs=[pl.BlockSpec((bb, tt, Wd), lambda b, i: (b, i, 0)), bspec(1, Wd)],
        out_shape=[jax.ShapeDtypeStruct((B, T, Wd), F32), jax.ShapeDtypeStruct((B, 1, Wd), F32)],
        scratch_shapes=[pltpu.VMEM((bb, tt + 8, Wd), F32), pltpu.VMEM((bb, 1, Wd), F32),
                        pltpu.VMEM((bb, tt, Wd), F32), pltpu.VMEM((bb, tt, Wd), F32)],
        compiler_params=_cparams(("arbitrary", "arbitrary")),
        name="lru",
    )(zxy, buf0, h0, W["conv_w"], W["conv_b"], W["w_ra_bd"], W["b_ra"], W["w_ri_bd"], W["b_ri"],
      W["lam"])


def _gmlp_kernel(zuv_ref, wcat_ref, bias_ref, g_ref, b_ref, o_ref, vn_ref):
    Wd = GM_WIDTH
    u = _gelu(zuv_ref[:, 0:Wd])
    v = _gelu(zuv_ref[:, Wd:2 * Wd])
    vc = v - jnp.mean(v, axis=-1, keepdims=True)
    vn = vc * lax.rsqrt(jnp.mean(vc * vc, axis=-1, keepdims=True) + EPS) * g_ref[...] + b_ref[...]
    vn_ref[...] = vn
    vb = vn.astype(BF16)
    grp = lax.broadcasted_iota(jnp.int32, vb.shape, 1) // GM_GD
    stacked = jnp.concatenate([jnp.where(grp == g, vb, jnp.zeros_like(vb)) for g in range(GM_GROUPS)],
                              axis=0)
    mixed = jnp.dot(wcat_ref[...], stacked, preferred_element_type=F32) + bias_ref[...]
    o_ref[...] = u * mixed


def _gmlp(zuv, wcat, bias, W):
    n = zuv.shape[0]
    Wd = GM_WIDTH
    return pl.pallas_call(
        _gmlp_kernel,
        grid=(n // CHUNK,),
        in_specs=[pl.BlockSpec((CHUNK, 2 * Wd), lambda i: (i, 0)),
                  _resident((CHUNK, GM_GROUPS * CHUNK)), _resident((CHUNK, Wd)),
                  _resident((1, Wd)), _resident((1, Wd))],
        out_specs=[pl.BlockSpec((CHUNK, Wd), lambda i: (i, 0)), pl.BlockSpec((CHUNK, Wd), lambda i: (i, 0))],
        out_shape=[jax.ShapeDtypeStruct((n, Wd), F32), jax.ShapeDtypeStruct((n, Wd), F32)],
        compiler_params=_cparams(("arbitrary",)),
        name="gmlp",
    )(zuv, wcat, bias, W["gm_g"], W["gm_b"])


def _merge_kernel(oa_ref, ob_ref, oc_ref, zg_ref, x_ref, gt_ref, sc_ref, sh_ref, wa_ref, wb_ref,
                  wc_ref, wo_ref, g_ref, x1_ref, h2_ref):
    D = D_MODEL
    merged = _sigmoid(zg_ref[0, :, 0:D]) * jnp.dot(oa_ref[0].astype(BF16), wa_ref[...],
                                                    preferred_element_type=F32)
    merged += _sigmoid(zg_ref[0, :, D:2 * D]) * jnp.dot(ob_ref[0].astype(BF16), wb_ref[...],
                                                         preferred_element_type=F32)
    merged += _sigmoid(zg_ref[0, :, 2 * D:3 * D]) * jnp.dot(oc_ref[0].astype(BF16), wc_ref[...],
                                                             preferred_element_type=F32)
    x1 = x_ref[0] + gt_ref[0] * jnp.dot(merged.astype(BF16), wo_ref[...], preferred_element_type=F32)
    x1_ref[0] = x1
    h2_ref[0] = (_rms(x1, g_ref[...]) * (1.0 + sc_ref[0]) + sh_ref[0]).astype(BF16)


def _merge(oa, ob, oc, zg, x, gt, sc, sh, W, tm):
    G, Tg, D = x.shape
    tok = lambda w: pl.BlockSpec((1, tm, w), lambda g_, i: (g_, i, 0))
    return pl.pallas_call(
        _merge_kernel,
        grid=(G, Tg // tm),
        in_specs=[tok(oa.shape[-1]), tok(LRU_WIDTH), tok(GM_WIDTH), tok(3 * D), tok(D),
                  _mod_spec(gt, tm), _mod_spec(sc, tm), _mod_spec(sh, tm),
                  _resident(W["w_br_a"].shape), _resident((LRU_WIDTH, D)), _resident((GM_WIDTH, D)),
                  _resident((D, D)), _resident((1, D))],
        out_specs=[tok(D), tok(D)],
        out_shape=[jax.ShapeDtypeStruct((G, Tg, D), F32), jax.ShapeDtypeStruct((G, Tg, D), BF16)],
        compiler_params=_cparams(("arbitrary", "arbitrary")),
        name="merge",
    )(oa, ob, oc, zg, x, gt, sc, sh, W["w_br_a"], W["w_br_b"], W["w_br_c"], W["w_out"], W["g_ffn"])


def _bitonic_sort_desc(v):
    n = len(v)
    v = list(v)
    k = 2
    while k <= n:
        j = k // 2
        while j >= 1:
            for i in range(n):
                l = i ^ j
                if l > i:
                    hi, lo = jnp.maximum(v[i], v[l]), jnp.minimum(v[i], v[l])
                    if (i & k) == 0:
                        v[i], v[l] = hi, lo
                    else:
                        v[i], v[l] = lo, hi
            j //= 2
        k *= 2
    return v


def _bitonic_merge_desc(v):
    n = len(v)
    v = list(v)
    j = n // 2
    while j >= 1:
        for i in range(n):
            l = i ^ j
            if l > i:
                v[i], v[l] = jnp.maximum(v[i], v[l]), jnp.minimum(v[i], v[l])
        j //= 2
    return v


def _top16_rows(s):
    v = _bitonic_sort_desc([s[SUBLANES * i:SUBLANES * (i + 1), :] for i in range(N_KEYS // SUBLANES)])
    for shift in (4, 2, 1):
        other = [pltpu.roll(x, shift, 0) for x in v]
        v = _bitonic_merge_desc([jnp.maximum(v[i], other[PEER_TOPK - 1 - i]) for i in range(PEER_TOPK)])
    return v


PEER_BROWS = 32

_STAIR = [(i, j) for i in range(PEER_TOPK) for j in range(PEER_TOPK) if (i + 1) * (j + 1) <= PEER_TOPK]


def _peer_kernel(h2_ref, x_ref, gt_ref, wpq_ref, sk_ref, eu_ref, evT_ref, o_ref,
                 s1_s, s2_s, e1_s, e2_s, tau_s, wT_s, acc_s, *, tm, ablk):
    j = pl.program_id(2)
    nj = pl.num_programs(2)
    H = PEER_HEADS

    @pl.when(j == 0)
    def _():
        q = jnp.dot(h2_ref[0], wpq_ref[...], preferred_element_type=F32).astype(BF16)
        sub = lax.broadcasted_iota(jnp.int32, (SUBLANES, tm), 0)
        T1 = [jnp.zeros((SUBLANES, tm), F32)] * PEER_TOPK
        T2 = [jnp.zeros((SUBLANES, tm), F32)] * PEER_TOPK
        for h in range(H):
            q1 = q[:, h * 2 * PEER_HALF:h * 2 * PEER_HALF + PEER_HALF]
            q2 = q[:, h * 2 * PEER_HALF + PEER_HALF:(h + 1) * 2 * PEER_HALF]
            s1 = lax.dot_general(sk_ref[h, 0], q1, NT_DIMS, preferred_element_type=F32)
            s2 = lax.dot_general(sk_ref[h, 1], q2, NT_DIMS, preferred_element_type=F32)
            s1_s[h] = s1
            s2_s[h] = s2
            t1 = _top16_rows(s1)
            t2 = _top16_rows(s2)
            T1 = [jnp.where(sub == h, t1[i], T1[i]) for i in range(PEER_TOPK)]
            T2 = [jnp.where(sub == h, t2[i], T2[i]) for i in range(PEER_TOPK)]
        cands = [T1[a] + T2[b] for (a, b) in _STAIR]
        m = T1[0] + T2[0]
        z = jnp.zeros((SUBLANES, tm), F32)
        top = m
        for it in range(PEER_TOPK):
            top = functools.reduce(jnp.maximum, cands)
            z = z + jnp.exp(top - m)
            if it + 1 < PEER_TOPK:
                cands = [jnp.where(c == top, NEG, c) for c in cands]
        tau_s[...] = top
        rz = 1.0 / z
        for h in range(H):
            e1_s[h] = jnp.exp(s1_s[h] - T1[0][h:h + 1, :]) * rz[h:h + 1, :]
            e2_s[h] = jnp.exp(s2_s[h] - T2[0][h:h + 1, :])
        acc_s[...] = jnp.zeros(acc_s.shape, F32)

    act = lax.dot_general(eu_ref[...], h2_ref[0], NT_DIMS, preferred_element_type=F32)
    base = pl.multiple_of(j * ablk, ablk)
    for lt in range(tm // LANES):
        ls = slice(lt * LANES, (lt + 1) * LANES)
        for bc in range(N_KEYS // PEER_BROWS):
            bs = slice(bc * PEER_BROWS, (bc + 1) * PEER_BROWS)
            g = [None] * ablk
            for h in range(H):
                s2c = s2_s[h, bs, ls]
                e2c = e2_s[h, bs, ls]
                tau = tau_s[h:h + 1, ls]
                s1t = s1_s[h, pl.ds(base, ablk), ls]
                e1t = e1_s[h, pl.ds(base, ablk), ls]
                for i in range(ablk):
                    s1a = s1t[i:i + 1, :]
                    e1a = e1t[i:i + 1, :]
                    term = jnp.where((s1a + s2c) >= tau, e2c, 0.0) * e1a
                    g[i] = term if h == 0 else g[i] + term
            for i in range(ablk):
                rows = slice(i * N_KEYS + bc * PEER_BROWS, i * N_KEYS + (bc + 1) * PEER_BROWS)
                wT_s[rows, ls] = (g[i] * _gelu(act[rows, ls])).astype(BF16)
    acc_s[...] += jnp.dot(evT_ref[...], wT_s[...], preferred_element_type=F32)

    @pl.when(j == nj - 1)
    def _():
        o_ref[0] = x_ref[0] + gt_ref[0] * acc_s[...].T


def _peer(h2, x1, gt, W, tm, ablk=8):
    G, Tg, D = x1.shape
    n_exp = N_KEYS * N_KEYS
    eb = ablk * N_KEYS
    tok = lambda w: pl.BlockSpec((1, tm, w), lambda g_, i, j: (g_, i, 0))
    if gt.shape[1] == 1:
        gt_spec = pl.BlockSpec((1, 1, D), lambda g_, i, j: (g_, 0, 0))
    else:
        gt_spec = tok(D)
    H = PEER_HEADS
    return pl.pallas_call(
        functools.partial(_peer_kernel, tm=tm, ablk=ablk),
        grid=(G, Tg // tm, n_exp // eb),
        in_specs=[tok(D), tok(D), gt_spec,
                  _resident((D, H * 2 * PEER_HALF)), _resident((H, 2, N_KEYS, PEER_HALF)),
                  pl.BlockSpec((eb, D), lambda g_, i, j: (j, 0)),
                  pl.BlockSpec((D, eb), lambda g_, i, j: (0, j))],
        out_specs=tok(D),
        out_shape=jax.ShapeDtypeStruct((G, Tg, D), F32),
        scratch_shapes=[pltpu.VMEM((H, N_KEYS, tm), F32), pltpu.VMEM((H, N_KEYS, tm), F32),
                        pltpu.VMEM((H, N_KEYS, tm), F32), pltpu.VMEM((H, N_KEYS, tm), F32),
                        pltpu.VMEM((SUBLANES, tm), F32), pltpu.VMEM((eb, tm), BF16),
                        pltpu.VMEM((D, tm), F32)],
        compiler_params=_cparams(("arbitrary", "arbitrary", "arbitrary")),
        name="peer",
    )(h2, x1, gt, W["w_pq"], W["sub_keys"], W["eu"], W["evT"])


def _pad_heads(w, per_head):
    lead = w.shape[:-1]
    w = w.reshape(lead + (N_HEADS, per_head))
    w = jnp.pad(w, [(0, 0)] * len(lead) + [(0, 0), (0, HEAD_PAD - per_head)])
    return w.reshape(lead + (N_HEADS * HEAD_PAD,))


def _block_diag(w):
    H, d, _ = w.shape
    eye = jnp.eye(H, dtype=w.dtype)
    return (eye[:, None, :, None] * w[:, :, None, :]).reshape(H * d, H * d)


def _prep_layer(l, w_in, g_norm_mix, g_norm_ffn, g_q_lora, w_uq, g_kv_lora, w_uk, w_uv, g_qn, g_kn,
                conv_w, conv_b, w_ra, b_ra, w_ri, b_ri, lru_lambda, gm_ln_g, gm_ln_b, w_s, b_s,
                w_branch, w_out, w_pq, sub_keys, expert_u, expert_v):
    D = D_MODEL
    wi = w_in[l]
    c0 = Q_LORA + KV_LORA
    zc = lambda n: jnp.zeros((D, n), F32)
    w_in_p = jnp.concatenate([wi[:, :c0], zc(QK_NOPE), wi[:, c0:c0 + QK_ROPE], zc(HEAD_PAD - QK_DIM),
                              wi[:, c0 + QK_ROPE:]], axis=1).astype(BF16)
    w_uk_pf = _pad_heads(w_uk[l], QK_NOPE)
    pad1 = lambda g: jnp.pad(g, (0, HEAD_PAD - g.shape[0])).reshape(1, HEAD_PAD)
    W = dict(
        w_in_p=w_in_p,
        g_mix=g_norm_mix[l].reshape(1, D), g_ffn=g_norm_ffn[l].reshape(1, D),
        g_q=g_q_lora[l].reshape(1, Q_LORA), g_kv=g_kv_lora[l].reshape(1, KV_LORA),
        w_uq_p=_pad_heads(w_uq[l], QK_DIM).astype(BF16),
        w_uk_p=w_uk_pf.astype(BF16), w_uk_pf=w_uk_pf,
        w_ukT=w_uk[l].T.astype(BF16),
        w_uv_p=_pad_heads(w_uv[l], V_DIM).astype(BF16),
        g_qn_p=pad1(g_qn[l]), g_kn_p=pad1(g_kn[l]),
        conv_w=conv_w[l], conv_b=conv_b[l].reshape(1, -1),
        w_ra_bd=_block_diag(w_ra[l]).astype(BF16), b_ra=b_ra[l].reshape(1, -1),
        w_ri_bd=_block_diag(w_ri[l]).astype(BF16), b_ri=b_ri[l].reshape(1, -1),
        lam=lru_lambda[l].reshape(1, -1),
        gm_g=gm_ln_g[l].reshape(1, -1), gm_b=gm_ln_b[l].reshape(1, -1),
        w_br_a=_pad_heads(w_branch[l, 0].T, V_DIM).T.astype(BF16),
        w_br_b=w_branch[l, 1].astype(BF16), w_br_c=w_branch[l, 2].astype(BF16),
        w_out=w_out[l].astype(BF16),
        w_pq=w_pq[l].astype(BF16), sub_keys=sub_keys[l].astype(BF16),
        eu=expert_u[l].astype(BF16), evT=expert_v[l].T.astype(BF16),
    )
    ws = w_s[l]
    bs = b_s[l]
    tril = jnp.tril(ws)
    W["gm_wcat_p"] = tril.transpose(1, 0, 2).reshape(CHUNK, GM_GROUPS * CHUNK).astype(BF16)
    W["gm_bias_p"] = jnp.repeat(bs.T, GM_GD, axis=1)
    return W, ws, bs


def _gmlp_sample_mats(ws, bs, ts):
    reps = CHUNK // ts
    small = jnp.tril(ws[:, :ts, :ts])
    eye = jnp.eye(reps, dtype=F32)
    big = (eye[None, :, None, :, None] * small[:, None, :, None, :]).reshape(GM_GROUPS, CHUNK, CHUNK)
    wcat = big.transpose(1, 0, 2).reshape(CHUNK, GM_GROUPS * CHUNK).astype(BF16)
    bias = jnp.tile(jnp.repeat(bs[:, :ts].T, GM_GD, axis=1), (reps, 1))
    return wcat, bias


def _rope_tables(pos):
    half = QK_ROPE // 2
    freq = ROPE_THETA ** (-jnp.arange(half, dtype=F32) / half)
    ang = pos.astype(F32)[:, None] * freq[None, :]
    cos, sin = jnp.cos(ang), jnp.sin(ang)
    n = pos.shape[0]
    z = lambda w: jnp.zeros((n, w), F32)
    c = jnp.concatenate([jnp.ones((n, QK_NOPE), F32), cos, cos, z(HEAD_PAD - QK_DIM)], axis=1)
    sa = jnp.concatenate([z(QK_NOPE + half), sin, z(HEAD_PAD - QK_DIM)], axis=1)
    sb = jnp.concatenate([z(QK_NOPE), -sin, z(half + HEAD_PAD - QK_DIM)], axis=1)
    return c, sa, sb


def _split_mod(mod):
    return [mod[..., k * D_MODEL:(k + 1) * D_MODEL] for k in range(6)]


def _layer_group(W, x, mods, tabs, attend, buf0, h0, gm_mats, tm, lru_bt, q_dtype):
    G, Tg, D = x.shape
    sh1, sc1, gt1, sh2, sc2, gt2 = mods
    zq, zkvr, zxy, zuv, zg = _inproj(x, sc1, sh1, W["g_mix"], W["w_in_p"], tm)
    q, k, v, rows = _qkprep(zq, zkvr, tabs, W, tm, q_dtype)
    o_a = attend(q, k, v, rows)
    B, T = buf0.shape[0], (G * Tg) // buf0.shape[0]
    zxy_b = zxy.reshape(B, T, 2 * LRU_WIDTH)
    bb, tt = lru_bt
    o_b, h_last = _lru(zxy_b, buf0, h0, W, bb, tt)
    conv_new = zxy_b[:, T - (CONV_WIDTH - 1):, :LRU_WIDTH]
    o_c, vn = _gmlp(zuv.reshape(G * Tg, 2 * GM_WIDTH), gm_mats[0], gm_mats[1], W)
    x1, h2 = _merge(o_a, o_b.reshape(G, Tg, -1), o_c.reshape(G, Tg, -1), zg, x, gt1, sc2, sh2, W, tm)
    xo = _peer(h2, x1, gt2, W, tm)
    return xo, rows, h_last.reshape(B, LRU_WIDTH), conv_new, vn


def kernel(x_prompt, x_sample, c_prompt, c_sample, cache_kv, state_lru_h, state_conv, page_table, w_ada, b_ada, g_norm_mix, g_norm_ffn, w_in, g_q_lora, w_uq, g_kv_lora, w_uk, w_uv, g_qn, g_kn, conv_w, conv_b, w_ra, b_ra, w_ri, b_ri, lru_lambda, gm_ln_g, gm_ln_b, w_s, b_s, w_branch, w_out, w_pq, sub_keys, expert_u, expert_v):
    Bp, Tp, D = x_prompt.shape
    Bs, Ts, _ = x_sample.shape
    n_past = page_table.shape[1] * cache_kv.shape[2]
    TM = 256
    cache_t = jnp.swapaxes(cache_kv, 2, 3)

    c_all = jnp.concatenate([c_prompt, c_sample], axis=0)
    nb = c_all.shape[0]
    nb_pad = -(-nb // SUBLANES) * SUBLANES
    c_all = jnp.pad(c_all, ((0, nb_pad - nb), (0, 0)))
    mod_all = _ada(c_all, w_ada, b_ada)

    tabs_p = _rope_tables(jnp.arange(Tp, dtype=F32))
    tabs_s = tuple(jnp.tile(t, (Bs, 1)) for t in _rope_tables(n_past + jnp.arange(Ts, dtype=F32)))

    xp = x_prompt
    xs = x_sample.reshape(1, Bs * Ts, D)
    buf0_p = jnp.zeros((Bp, CONV_WIDTH - 1, LRU_WIDTH), F32)
    h0_p = jnp.zeros((Bp, 1, LRU_WIDTH), F32)
    outs = [[] for _ in range(8)]
    for l in range(DEPTH):
        W, ws, bs = _prep_layer(l, w_in, g_norm_mix, g_norm_ffn, g_q_lora, w_uq, g_kv_lora, w_uk, w_uv,
                                g_qn, g_kn, conv_w, conv_b, w_ra, b_ra, w_ri, b_ri, lru_lambda, gm_ln_g,
                                gm_ln_b, w_s, b_s, w_branch, w_out, w_pq, sub_keys, expert_u, expert_v)
        mods_p = [m[:, None, :] for m in _split_mod(mod_all[l, :Bp])]
        mods_s = [jnp.repeat(m, Ts, axis=0)[None] for m in _split_mod(mod_all[l, Bp:Bp + Bs])]

        def attend_p(q, k, v, rows):
            return _flash(q, k, v)

        def attend_s(q, k, v, rows, W=W, l=l):
            qabs, qg = _qabs(q.reshape(Bs * Ts, -1), W)
            qabs = qabs.reshape(Bs, Ts, N_HEADS, KV_LORA).transpose(0, 2, 1, 3)
            qabs = qabs.reshape(Bs, N_HEADS * Ts, KV_LORA).astype(BF16)
            qr = qg.reshape(Bs, Ts, N_HEADS, HEAD_PAD)[..., QK_NOPE:QK_DIM].transpose(0, 2, 1, 3)
            qr = qr.reshape(Bs, N_HEADS * Ts, QK_ROPE).astype(BF16)
            o = _paged(page_table, qabs, qr, rows.reshape(Bs, Ts, KV_ROW), cache_t, W, l)
            return o.reshape(1, Bs * Ts, -1)

        xp, rp, hp, cp, vp = _layer_group(W, xp, mods_p, tabs_p, attend_p, buf0_p, h0_p,
                                          (W["gm_wcat_p"], W["gm_bias_p"]), TM, (Bp, 256), BF16)
        xs, rs, hs, cs, vs = _layer_group(W, xs, mods_s, tabs_s, attend_s, state_conv[:, l],
                                          state_lru_h[:, l][:, None, :], _gmlp_sample_mats(ws, bs, Ts),
                                          TM, (16, Ts), F32)
        outs[0].append(rp)
        outs[1].append(rs.reshape(Bs, Ts, KV_ROW))
        outs[2].append(hp)
        outs[3].append(hs)
        outs[4].append(cp)
        outs[5].append(cs)
        outs[6].append(vp.reshape(Bp, Tp, GM_WIDTH)[:, -CHUNK:])
        outs[7].append(vs.reshape(Bs, Ts, GM_WIDTH))
    stacked = [jnp.stack(o, axis=1) for o in outs]
    return (xp, xs.reshape(Bs, Ts, D), *stacked)
```

```python
import functools
import math

import numpy as np
import jax
import jax.numpy as jnp
from jax import lax
from jax.experimental import pallas as pl
from jax.experimental.pallas import tpu as pltpu

F32 = jnp.float32
BF16 = jnp.bfloat16

D_MODEL = 1024
DEPTH = 2
N_HEADS = 8
Q_LORA = 384
KV_LORA = 256
QK_NOPE = 64
QK_ROPE = 32
QK_DIM = QK_NOPE + QK_ROPE
V_DIM = 64
KV_ROW = KV_LORA + QK_ROPE
ROPE_THETA = 10000.0
LRU_WIDTH = 512
LRU_HEADS = 8
LRU_HD = LRU_WIDTH // LRU_HEADS
CONV_WIDTH = 4
RG_C = 8.0
GM_WIDTH = 512
GM_GROUPS = 8
GM_GD = GM_WIDTH // GM_GROUPS
CHUNK = 128
PEER_HEADS = 8
N_KEYS = 128
PEER_HALF = 128
PEER_TOPK = 16
EPS = 1e-6
NEG = -1e30

LANES = 128
SUBLANES = 8
HEAD_PAD = LANES
VMEM_LIMIT = 56 * 1024 * 1024

NT_DIMS = (((1,), (1,)), ((), ()))


def _cparams(sem):
    return pltpu.CompilerParams(dimension_semantics=sem, vmem_limit_bytes=VMEM_LIMIT)


def _resident(shape):
    nd = len(shape)
    return pl.BlockSpec(shape, lambda *_: (0,) * nd, pipeline_mode=pl.Buffered(1))


def _gelu(x):
    return 0.5 * x * (1.0 + lax.erf(x * (2.0 ** -0.5)))


def _sigmoid(x):
    return jax.nn.sigmoid(x)


def _rms(x, g, n=None):
    n = x.shape[-1] if n is None else n
    ss = jnp.sum(x * x, axis=-1, keepdims=True)
    return x * lax.rsqrt(ss * (1.0 / n) + EPS) * g


def _ada_kernel(c_ref, w_ref, b_ref, o_ref):
    c = c_ref[...]
    s = (c * _sigmoid(c)).astype(BF16)
    o_ref[0] = jnp.dot(s, w_ref[0].astype(BF16), preferred_element_type=F32) + b_ref[0]


def _ada(c_all, w_ada, b_ada):
    nb = c_all.shape[0]
    tn = 512
    ncol = w_ada.shape[-1]
    return pl.pallas_call(
        _ada_kernel,
        grid=(DEPTH, ncol // tn),
        in_specs=[
            pl.BlockSpec((nb, D_MODEL), lambda l, j: (0, 0)),
            pl.BlockSpec((1, D_MODEL, tn), lambda l, j: (l, 0, j)),
            pl.BlockSpec((1, 1, tn), lambda l, j: (l, 0, j)),
        ],
        out_specs=pl.BlockSpec((1, nb, tn), lambda l, j: (l, 0, j)),
        out_shape=jax.ShapeDtypeStruct((DEPTH, nb, ncol), F32),
        compiler_params=_cparams(("arbitrary", "arbitrary")),
        name="ada",
    )(c_all, w_ada, b_ada.reshape(DEPTH, 1, ncol))


def _mod_spec(mod, tm):
    if mod.shape[1] == 1:
        return pl.BlockSpec((1, 1, mod.shape[2]), lambda g, i: (g, 0, 0))
    return pl.BlockSpec((1, tm, mod.shape[2]), lambda g, i: (g, i, 0))


Z_SPLITS = (Q_LORA, 384, 2 * LRU_WIDTH, 2 * GM_WIDTH, 3 * D_MODEL)
Z_COLS = sum(Z_SPLITS)


def _inproj_kernel(x_ref, sc_ref, sh_ref, g_ref, w_ref, *o_refs):
    x = x_ref[0]
    h = _rms(x, g_ref[...]) * (1.0 + sc_ref[0]) + sh_ref[0]
    hb = h.astype(BF16)
    off = 0
    for o_ref, width in zip(o_refs, Z_SPLITS):
        o_ref[0] = jnp.dot(hb, w_ref[:, off:off + width], preferred_element_type=F32)
        off += width


def _inproj(x, sc, sh, g, w_in_p, tm):
    G, Tg, D = x.shape
    out_shape = [jax.ShapeDtypeStruct((G, Tg, w), F32) for w in Z_SPLITS]
    out_specs = [pl.BlockSpec((1, tm, w), lambda g_, i: (g_, i, 0)) for w in Z_SPLITS]
    return pl.pallas_call(
        _inproj_kernel,
        grid=(G, Tg // tm),
        in_specs=[
            pl.BlockSpec((1, tm, D), lambda g_, i: (g_, i, 0)),
            _mod_spec(sc, tm), _mod_spec(sh, tm),
            _resident((1, D)), _resident((D, Z_COLS)),
        ],
        out_specs=out_specs,
        out_shape=out_shape,
        compiler_params=_cparams(("arbitrary", "arbitrary")),
        name="inproj",
    )(x, sc, sh, g, w_in_p)


def _rope_block(blk, c, sa, sb):
    n = blk.shape[-1]
    return blk * c + pltpu.roll(blk, 16, 1) * sa + pltpu.roll(blk, n - 16, 1) * sb


def _qkprep_kernel(zq_ref, zkvr_ref, c_ref, sa_ref, sb_ref, gq_ref, wuq_ref, gkv_ref, wuk_ref,
                   wuv_ref, gqn_ref, gkn_ref, q_ref, k_ref, v_ref, rows_ref):
    c, sa, sb = c_ref[...], sa_ref[...], sb_ref[...]
    qn = _rms(zq_ref[0], gq_ref[...]).astype(BF16)
    qf = jnp.dot(qn, wuq_ref[...], preferred_element_type=F32)
    gqn, gkn = gqn_ref[...], gkn_ref[...]
    for h in range(N_HEADS):
        sl = slice(h * HEAD_PAD, (h + 1) * HEAD_PAD)
        r = _rope_block(qf[:, sl], c, sa, sb)
        q_ref[0, :, sl] = _rms(r, gqn, QK_DIM).astype(q_ref.dtype)
    zkvr = zkvr_ref[0]
    lat = _rms(zkvr[:, :KV_LORA], gkv_ref[...])
    krb = _rope_block(zkvr[:, KV_LORA:], c, sa, sb)
    rows_ref[0, :, 0:KV_LORA] = lat
    rows_ref[0, :, KV_LORA:KV_ROW] = krb[:, QK_NOPE:QK_DIM]
    latb = lat.astype(BF16)
    knp = jnp.dot(latb, wuk_ref[...], preferred_element_type=F32)
    for h in range(N_HEADS):
        sl = slice(h * HEAD_PAD, (h + 1) * HEAD_PAD)
        k_ref[0, :, sl] = _rms(knp[:, sl] + krb, gkn, QK_DIM).astype(BF16)
    v = jnp.dot(latb, wuv_ref[...], preferred_element_type=F32)
    one_lane = lax.broadcasted_iota(jnp.int32, v.shape, 1) % HEAD_PAD == V_DIM
    v_ref[0] = jnp.where(one_lane, 1.0, v).astype(BF16)


def _qkprep(zq, zkvr, tabs, W, tm, q_dtype):
    G, Tg, _ = zq.shape
    c, sa, sb = tabs
    HP = N_HEADS * HEAD_PAD
    tok = lambda w: pl.BlockSpec((1, tm, w), lambda g_, i: (g_, i, 0))
    tab = pl.BlockSpec((tm, HEAD_PAD), lambda g_, i: (i, 0))
    return pl.pallas_call(
        _qkprep_kernel,
        grid=(G, Tg // tm),
        in_specs=[tok(Q_LORA), tok(384), tab, tab, tab,
                  _resident((1, Q_LORA)), _resident((Q_LORA, HP)), _resident((1, KV_LORA)),
                  _resident((KV_LORA, HP)), _resident((KV_LORA, HP)),
                  _resident((1, HEAD_PAD)), _resident((1, HEAD_PAD))],
        out_specs=[tok(HP), tok(HP), tok(HP), tok(KV_ROW)],
        out_shape=[jax.ShapeDtypeStruct((G, Tg, HP), q_dtype),
                   jax.ShapeDtypeStruct((G, Tg, HP), BF16),
                   jax.ShapeDtypeStruct((G, Tg, HP), BF16),
                   jax.ShapeDtypeStruct((G, Tg, KV_ROW), F32)],
        compiler_params=_cparams(("arbitrary", "arbitrary")),
        name="qkprep",
    )(zq, zkvr, c, sa, sb, W["g_q"], W["w_uq_p"], W["g_kv"], W["w_uk_p"], W["w_uv_p"],
      W["g_qn_p"], W["g_kn_p"])


def _flash_kernel(q_ref, k_ref, v_ref, o_ref, m_ref, acc_ref, *, tq, tk):
    qi, ki = pl.program_id(1), pl.program_id(2)
    c = (QK_DIM ** -0.5) * math.log2(math.e)

    @pl.when(ki == 0)
    def _():
        m_ref[...] = jnp.full(m_ref.shape, NEG, F32)
        acc_ref[...] = jnp.zeros(acc_ref.shape, F32)

    def step(diagonal):
        if diagonal:
            causal = (lax.broadcasted_iota(jnp.int32, (tq, tk), 0)
                      >= lax.broadcasted_iota(jnp.int32, (tq, tk), 1))
        for h in range(N_HEADS):
            sl = slice(h * HEAD_PAD, (h + 1) * HEAD_PAD)
            s = lax.dot_general(q_ref[0, :, sl], k_ref[0, :, sl], NT_DIMS, preferred_element_type=F32)
            if diagonal:
                s = jnp.where(causal, s, NEG)
            m_prev = m_ref[h]
            m_new = jnp.maximum(m_prev, jnp.max(s, axis=-1, keepdims=True))
            alpha = jnp.exp2((m_prev - m_new) * c)
            p = jnp.exp2((s - m_new) * c)
            acc_ref[h] = alpha * acc_ref[h] + jnp.dot(p.astype(BF16), v_ref[0, :, sl],
                                                      preferred_element_type=F32)
            m_ref[h] = m_new

    @pl.when(ki < qi)
    def _():
        step(False)

    @pl.when(ki == qi)
    def _():
        step(True)
        for h in range(N_HEADS):
            sl = slice(h * HEAD_PAD, (h + 1) * HEAD_PAD)
            acc = acc_ref[h]
            o_ref[0, :, sl] = (acc / acc[:, V_DIM:V_DIM + 1]).astype(o_ref.dtype)


def _flash(q, k, v, tq=512):
    B, T, HP = q.shape
    tk = tq
    kv_spec = pl.BlockSpec((1, tk, HP), lambda b, i, j: (b, jnp.minimum(i, j), 0))
    return pl.pallas_call(
        functools.partial(_flash_kernel, tq=tq, tk=tk),
        grid=(B, T // tq, T // tk),
        in_specs=[pl.BlockSpec((1, tq, HP), lambda b, i, j: (b, i, 0)), kv_spec, kv_spec],
        out_specs=pl.BlockSpec((1, tq, HP), lambda b, i, j: (b, i, 0)),
        out_shape=jax.ShapeDtypeStruct((B, T, HP), BF16),
        scratch_shapes=[pltpu.VMEM((N_HEADS, tq, 1), F32), pltpu.VMEM((N_HEADS, tq, HEAD_PAD), F32)],
        compiler_params=_cparams(("arbitrary", "arbitrary", "arbitrary")),
        name="flash",
    )(q, k, v)


def _qabs_kernel(q_ref, gkn_ref, wuk_ref, qabs_ref, qg_ref):
    g = gkn_ref[...]
    for h in range(N_HEADS):
        sl = slice(h * HEAD_PAD, (h + 1) * HEAD_PAD)
        qg = q_ref[:, sl] * g
        qg_ref[:, sl] = qg
        qabs_ref[:, h * KV_LORA:(h + 1) * KV_LORA] = lax.dot_general(
            qg, wuk_ref[:, sl], NT_DIMS, preferred_element_type=F32,
            precision=lax.Precision.HIGHEST)


def _qabs(q2d, W):
    n, HP = q2d.shape
    tm = 256
    return pl.pallas_call(
        _qabs_kernel,
        grid=(n // tm,),
        in_specs=[pl.BlockSpec((tm, HP), lambda i: (i, 0)), _resident((1, HEAD_PAD)),
                  _resident((KV_LORA, HP))],
        out_specs=[pl.BlockSpec((tm, N_HEADS * KV_LORA), lambda i: (i, 0)),
                   pl.BlockSpec((tm, HP), lambda i: (i, 0))],
        out_shape=[jax.ShapeDtypeStruct((n, N_HEADS * KV_LORA), F32),
                   jax.ShapeDtypeStruct((n, HP), F32)],
        compiler_params=_cparams(("arbitrary",)),
        name="qabs",
    )(q2d, W["g_kn_p"], W["w_uk_pf"])


PAGES_PER_STEP = 32


def _paged_kernel(pt_ref, qabs_ref, qr_ref, newT_ref, wukT_ref, wuv_ref, cache_ref, o_ref,
                  buf, sem, lhs, m_ref, l_ref, acc_ref, newbuf, *, layer, n_seq, n_chunks, page):
    P = PAGES_PER_STEP
    s_id, c_id = pl.program_id(0), pl.program_id(1)
    step = s_id * n_chunks + c_id
    total = n_seq * n_chunks
    nq = newT_ref.shape[2]
    HQ = N_HEADS * nq
    scale = QK_DIM ** -0.5

    def page_copy(slot, p, page_id):
        return pltpu.make_async_copy(cache_ref.at[page_id, layer], buf.at[slot, p], sem.at[slot, p])

    def start_fetch(step_, slot):
        for p in range(P):
            page_copy(slot, p, pt_ref[step_ * P + p]).start()

    def wait_fetch(slot):
        for p in range(P):
            page_copy(slot, p, 0).wait()

    @pl.when(step == 0)
    def _():
        lhs[0:N_HEADS * QK_NOPE, :] = wukT_ref[...]
        start_fetch(0, 0)

    @pl.when(step + 1 < total)
    def _():
        start_fetch(step + 1, (step + 1) % 2)

    @pl.when(c_id == 0)
    def _():
        lhs[N_HEADS * QK_NOPE:, :] = qabs_ref[0]
        m_ref[...] = jnp.full(m_ref.shape, NEG, F32)
        l_ref[...] = jnp.zeros(l_ref.shape, F32)
        acc_ref[...] = jnp.zeros(acc_ref.shape, F32)

    def process(latT, krT, mask):
        big = jnp.dot(lhs[...], latT, preferred_element_type=F32)
        sc = big[N_HEADS * QK_NOPE:] + jnp.dot(qr_ref[0], krT.astype(BF16), preferred_element_type=F32)
        ssk = jnp.sum(krT * krT, axis=0, keepdims=True)
        parts = []
        for h in range(N_HEADS):
            kn = big[h * QK_NOPE:(h + 1) * QK_NOPE]
            ss = jnp.sum(kn * kn, axis=0, keepdims=True) + ssk
            rs = lax.rsqrt(ss * (1.0 / QK_DIM) + EPS)
            parts.append(sc[h * nq:(h + 1) * nq] * rs)
        s = jnp.concatenate(parts, axis=0) * scale
        if mask is not None:
            s = jnp.where(mask, s, NEG)
        m_prev = m_ref[...]
        m_new = jnp.maximum(m_prev, jnp.max(s, axis=-1, keepdims=True))
        alpha = jnp.exp(m_prev - m_new)
        p = jnp.exp(s - m_new)
        l_ref[...] = alpha * l_ref[...] + jnp.sum(p, axis=-1, keepdims=True)
        acc_ref[...] = alpha * acc_ref[...] + lax.dot_general(p.astype(BF16), latT, NT_DIMS,
                                                              preferred_element_type=F32)
        m_ref[...] = m_new

    slot = step % 2
    wait_fetch(slot)
    latT = jnp.concatenate([buf[slot, p, 0:KV_LORA, :].astype(BF16) for p in range(P)], axis=1)
    krT = jnp.concatenate([buf[slot, p, KV_LORA:KV_ROW, :] for p in range(P)], axis=1)
    process(latT, krT, None)

    @pl.when(c_id == n_chunks - 1)
    def _():
        newbuf[...] = jnp.zeros(newbuf.shape, F32)
        newbuf[:, 0:nq] = newT_ref[0]
        key = lax.broadcasted_iota(jnp.int32, (HQ, newbuf.shape[1]), 1)
        qpos = lax.broadcasted_iota(jnp.int32, (HQ, newbuf.shape[1]), 0) % nq
        process(newbuf[0:KV_LORA, :].astype(BF16), newbuf[KV_LORA:KV_ROW, :], key <= qpos)
        for h in range(N_HEADS):
            ctx = (acc_ref[h * nq:(h + 1) * nq] / l_ref[h * nq:(h + 1) * nq]).astype(BF16)
            sl = slice(h * HEAD_PAD, (h + 1) * HEAD_PAD)
            o_ref[0, :, sl] = jnp.dot(ctx, wuv_ref[:, sl], preferred_element_type=F32)


def _paged(page_table, qabs, qr, new_rows, cache_t, W, layer):
    n_seq, n_pages = page_table.shape
    page = cache_t.shape[3]
    nq = new_rows.shape[1]
    new_t = jnp.swapaxes(new_rows, 1, 2)
    HQ = N_HEADS * nq
    HP = N_HEADS * HEAD_PAD
    P = PAGES_PER_STEP
    n_chunks = n_pages // P
    grid_spec = pltpu.PrefetchScalarGridSpec(
        num_scalar_prefetch=1,
        grid=(n_seq, n_chunks),
        in_specs=[
            pl.BlockSpec((1, HQ, KV_LORA), lambda s, c, pt: (s, 0, 0)),
            pl.BlockSpec((1, HQ, QK_ROPE), lambda s, c, pt: (s, 0, 0)),
            pl.BlockSpec((1, KV_ROW, nq), lambda s, c, pt: (s, 0, 0)),
            pl.BlockSpec((N_HEADS * QK_NOPE, KV_LORA), lambda s, c, pt: (0, 0)),
            pl.BlockSpec((KV_LORA, HP), lambda s, c, pt: (0, 0)),
            pl.BlockSpec(memory_space=pl.ANY),
        ],
        out_specs=pl.BlockSpec((1, nq, HP), lambda s, c, pt: (s, 0, 0)),
        scratch_shapes=[
            pltpu.VMEM((2, P, KV_ROW, page), F32),
            pltpu.SemaphoreType.DMA((2, P)),
            pltpu.VMEM((N_HEADS * QK_NOPE + HQ, KV_LORA), BF16),
            pltpu.VMEM((HQ, 1), F32), pltpu.VMEM((HQ, 1), F32), pltpu.VMEM((HQ, KV_LORA), F32),
            pltpu.VMEM((KV_ROW, LANES), F32),
        ],
    )
    return pl.pallas_call(
        functools.partial(_paged_kernel, layer=layer, n_seq=n_seq, n_chunks=n_chunks, page=page),
        grid_spec=grid_spec,
        out_shape=jax.ShapeDtypeStruct((n_seq, nq, HP), F32),
        compiler_params=_cparams(("arbitrary", "arbitrary")),
        name="paged",
    )(page_table.reshape(-1), qabs, qr, new_t, W["w_ukT"], W["w_uv_p"], cache_t)


def _expm1(x):
    u = jnp.exp(x)
    safe = jnp.where(u == 1.0, 2.0, jnp.where(u == 0.0, 0.5, u))
    val = (u - 1.0) * x / jnp.log(safe)
    return jnp.where(u == 1.0, x, jnp.where(u == 0.0, -1.0, val))


def _lru_kernel(zxy_ref, buf0_ref, h0_ref, cw_ref, cb_ref, wra_ref, bra_ref, wri_ref, bri_ref,
                lam_ref, o_ref, hl_ref, xbuf, hc, a_s, b_s, *, bb, tt):
    W = LRU_WIDTH
    ti = pl.program_id(1)

    @pl.when(ti == 0)
    def _():
        xbuf[:, 5:8, :] = buf0_ref[...]
        hc[...] = h0_ref[...]

    xbuf[:, 8:8 + tt, :] = zxy_ref[:, :, 0:W]
    xc = cb_ref[...] + cw_ref[0:1, :] * xbuf[:, 5:5 + tt, :]
    for j in range(1, CONV_WIDTH):
        xc = xc + cw_ref[j:j + 1, :] * xbuf[:, 5 + j:5 + j + tt, :]
    xbuf[:, 5:8, :] = xbuf[:, 5 + tt:8 + tt, :]
    xc2 = xc.reshape(bb * tt, W)
    xcb = xc2.astype(BF16)
    r = _sigmoid(jnp.dot(xcb, wra_ref[...], preferred_element_type=F32) + bra_ref[...])
    i = _sigmoid(jnp.dot(xcb, wri_ref[...], preferred_element_type=F32) + bri_ref[...])
    nl = -lam_ref[...]
    softplus = jnp.maximum(nl, 0.0) + jnp.log1p(jnp.exp(-jnp.abs(nl)))
    log_a = -RG_C * r * softplus
    a_s[...] = jnp.exp(log_a).reshape(bb, tt, W)
    b_s[...] = (jnp.sqrt(-_expm1(2.0 * log_a)) * (i * xc2)).reshape(bb, tt, W)

    def body(g, h):
        base = pl.multiple_of(g * SUBLANES, SUBLANES)
        at = a_s[:, pl.ds(base, SUBLANES), :]
        bt = b_s[:, pl.ds(base, SUBLANES), :]
        for k in range(SUBLANES):
            h = at[:, k:k + 1, :] * h + bt[:, k:k + 1, :]
            a_s[:, pl.ds(base + k, 1), :] = h
        return h

    h = lax.fori_loop(0, tt // SUBLANES, body, hc[...])
    hc[...] = h
    hl_ref[...] = h
    o_ref[...] = a_s[...] * _gelu(zxy_ref[:, :, W:2 * W])


def _lru(zxy, buf0, h0, W, bb, tt):
    B, T, _ = zxy.shape
    Wd = LRU_WIDTH
    bspec = lambda t, w: pl.BlockSpec((bb, t, w), lambda b, i: (b, 0, 0))
    return pl.pallas_call(
        functools.partial(_lru_kernel, bb=bb, tt=tt),
        grid=(B // bb, T // tt),
        in_specs=[pl.BlockSpec((bb, tt, 2 * Wd), lambda b, i: (b, i, 0)),
                  bspec(CONV_WIDTH - 1, Wd), bspec(1, Wd),
                  _resident((CONV_WIDTH, Wd)), _resident((1, Wd)),
                  _resident((Wd, Wd)), _resident((1, Wd)), _resident((Wd, Wd)), _resident((1, Wd)),
                  _resident((1, Wd))],
        out_specs=[pl.BlockSpec((bb, tt, Wd), lambda b, i: (b, i, 0)), bspec(1, Wd)],
        out_shape=[jax.ShapeDtypeStruct((B, T, Wd), F32), jax.ShapeDtypeStruct((B, 1, Wd), F32)],
        scratch_shapes=[pltpu.VMEM((bb, tt + 8, Wd), F32), pltpu.VMEM((bb, 1, Wd), F32),
                        pltpu.VMEM((bb, tt, Wd), F32), pltpu.VMEM((bb, tt, Wd), F32)],
        compiler_params=_cparams(("arbitrary", "arbitrary")),
        name="lru",
    )(zxy, buf0, h0, W["conv_w"], W["conv_b"], W["w_ra_bd"], W["b_ra"], W["w_ri_bd"], W["b_ri"],
      W["lam"])


def _gmlp_kernel(zuv_ref, wcat_ref, bias_ref, g_ref, b_ref, o_ref, vn_ref):
    Wd = GM_WIDTH
    u = _gelu(zuv_ref[:, 0:Wd])
    v = _gelu(zuv_ref[:, Wd:2 * Wd])
    vc = v - jnp.mean(v, axis=-1, keepdims=True)
    vn = vc * lax.rsqrt(jnp.mean(vc * vc, axis=-1, keepdims=True) + EPS) * g_ref[...] + b_ref[...]
    vn_ref[...] = vn
    vb = vn.astype(BF16)
    grp = lax.broadcasted_iota(jnp.int32, vb.shape, 1) // GM_GD
    stacked = jnp.concatenate([jnp.where(grp == g, vb, jnp.zeros_like(vb)) for g in range(GM_GROUPS)],
                              axis=0)
    mixed = jnp.dot(wcat_ref[...], stacked, preferred_element_type=F32) + bias_ref[...]
    o_ref[...] = u * mixed


def _gmlp(zuv, wcat, bias, W):
    n = zuv.shape[0]
    Wd = GM_WIDTH
    return pl.pallas_call(
        _gmlp_kernel,
        grid=(n // CHUNK,),
        in_specs=[pl.BlockSpec((CHUNK, 2 * Wd), lambda i: (i, 0)),
                  _resident((CHUNK, GM_GROUPS * CHUNK)), _resident((CHUNK, Wd)),
                  _resident((1, Wd)), _resident((1, Wd))],
        out_specs=[pl.BlockSpec((CHUNK, Wd), lambda i: (i, 0)), pl.BlockSpec((CHUNK, Wd), lambda i: (i, 0))],
        out_shape=[jax.ShapeDtypeStruct((n, Wd), F32), jax.ShapeDtypeStruct((n, Wd), F32)],
        compiler_params=_cparams(("arbitrary",)),
        name="gmlp",
    )(zuv, wcat, bias, W["gm_g"], W["gm_b"])


def _merge_kernel(oa_ref, ob_ref, oc_ref, zg_ref, x_ref, gt_ref, sc_ref, sh_ref, wa_ref, wb_ref,
                  wc_ref, wo_ref, g_ref, x1_ref, h2_ref):
    D = D_MODEL
    merged = _sigmoid(zg_ref[0, :, 0:D]) * jnp.dot(oa_ref[0].astype(BF16), wa_ref[...],
                                                    preferred_element_type=F32)
    merged += _sigmoid(zg_ref[0, :, D:2 * D]) * jnp.dot(ob_ref[0].astype(BF16), wb_ref[...],
                                                         preferred_element_type=F32)
    merged += _sigmoid(zg_ref[0, :, 2 * D:3 * D]) * jnp.dot(oc_ref[0].astype(BF16), wc_ref[...],
                                                             preferred_element_type=F32)
    x1 = x_ref[0] + gt_ref[0] * jnp.dot(merged.astype(BF16), wo_ref[...], preferred_element_type=F32)
    x1_ref[0] = x1
    h2_ref[0] = (_rms(x1, g_ref[...]) * (1.0 + sc_ref[0]) + sh_ref[0]).astype(BF16)


def _merge(oa, ob, oc, zg, x, gt, sc, sh, W, tm):
    G, Tg, D = x.shape
    tok = lambda w: pl.BlockSpec((1, tm, w), lambda g_, i: (g_, i, 0))
    return pl.pallas_call(
        _merge_kernel,
        grid=(G, Tg // tm),
        in_specs=[tok(oa.shape[-1]), tok(LRU_WIDTH), tok(GM_WIDTH), tok(3 * D), tok(D),
                  _mod_spec(gt, tm), _mod_spec(sc, tm), _mod_spec(sh, tm),
                  _resident(W["w_br_a"].shape), _resident((LRU_WIDTH, D)), _resident((GM_WIDTH, D)),
                  _resident((D, D)), _resident((1, D))],
        out_specs=[tok(D), tok(D)],
        out_shape=[jax.ShapeDtypeStruct((G, Tg, D), F32), jax.ShapeDtypeStruct((G, Tg, D), BF16)],
        compiler_params=_cparams(("arbitrary", "arbitrary")),
        name="merge",
    )(oa, ob, oc, zg, x, gt, sc, sh, W["w_br_a"], W["w_br_b"], W["w_br_c"], W["w_out"], W["g_ffn"])


def _bitonic_sort_desc(v):
    n = len(v)
    v = list(v)
    k = 2
    while k <= n:
        j = k // 2
        while j >= 1:
            for i in range(n):
                l = i ^ j
                if l > i:
                    hi, lo = jnp.maximum(v[i], v[l]), jnp.minimum(v[i], v[l])
                    if (i & k) == 0:
                        v[i], v[l] = hi, lo
                    else:
                        v[i], v[l] = lo, hi
            j //= 2
        k *= 2
    return v


def _bitonic_merge_desc(v):
    n = len(v)
    v = list(v)
    j = n // 2
    while j >= 1:
        for i in range(n):
            l = i ^ j
            if l > i:
                v[i], v[l] = jnp.maximum(v[i], v[l]), jnp.minimum(v[i], v[l])
        j //= 2
    return v


def _top16_rows(s):
    v = _bitonic_sort_desc([s[SUBLANES * i:SUBLANES * (i + 1), :] for i in range(N_KEYS // SUBLANES)])
    for shift in (4, 2, 1):
        other = [pltpu.roll(x, shift, 0) for x in v]
        v = _bitonic_merge_desc([jnp.maximum(v[i], other[PEER_TOPK - 1 - i]) for i in range(PEER_TOPK)])
    return v


PEER_BVREGS = 2
PEER_AGRP = 4

_STAIR = [(i, j) for i in range(PEER_TOPK) for j in range(PEER_TOPK) if (i + 1) * (j + 1) <= PEER_TOPK]


def _peer_kernel(h2_ref, x_ref, gt_ref, wpq_ref, sk_ref, eu_ref, evT_ref, o_ref,
                 s1_s, s2_s, e2_s, s1b_s, e1b_s, taub_s, wT_s, acc_s, *, tm, ablk):
    j = pl.program_id(2)
    nj = pl.num_programs(2)
    H = PEER_HEADS

    @pl.when(j == 0)
    def _():
        q = jnp.dot(h2_ref[0], wpq_ref[...], preferred_element_type=F32).astype(BF16)
        sub = lax.broadcasted_iota(jnp.int32, (SUBLANES, tm), 0)
        T1 = [jnp.zeros((SUBLANES, tm), F32)] * PEER_TOPK
        T2 = [jnp.zeros((SUBLANES, tm), F32)] * PEER_TOPK
        for h in range(H):
            q1 = q[:, h * 2 * PEER_HALF:h * 2 * PEER_HALF + PEER_HALF]
            q2 = q[:, h * 2 * PEER_HALF + PEER_HALF:(h + 1) * 2 * PEER_HALF]
            s1 = lax.dot_general(sk_ref[h, 0], q1, NT_DIMS, preferred_element_type=F32)
            s2 = lax.dot_general(sk_ref[h, 1], q2, NT_DIMS, preferred_element_type=F32)
            s1_s[h] = s1
            s2_s[h] = s2
            t1 = _top16_rows(s1)
            t2 = _top16_rows(s2)
            T1 = [jnp.where(sub == h, t1[i], T1[i]) for i in range(PEER_TOPK)]
            T2 = [jnp.where(sub == h, t2[i], T2[i]) for i in range(PEER_TOPK)]
        cands = [T1[a] + T2[b] for (a, b) in _STAIR]
        m = T1[0] + T2[0]
        z = jnp.zeros((SUBLANES, tm), F32)
        top = m
        for it in range(PEER_TOPK):
            top = functools.reduce(jnp.maximum, cands)
            z = z + jnp.exp(top - m)
            if it + 1 < PEER_TOPK:
                cands = [jnp.where(c == top, NEG, c) for c in cands]
        rz = 1.0 / z
        for h in range(H):
            e2_s[h] = jnp.exp(s2_s[h] - T2[0][h:h + 1, :])
            taub_s[h] = jnp.broadcast_to(top[h:h + 1, :], (SUBLANES, tm))
            m1 = jnp.broadcast_to(T1[0][h:h + 1, :], (SUBLANES, tm))
            rzh = jnp.broadcast_to(rz[h:h + 1, :], (SUBLANES, tm))

            def bcast_rows(g, carry, h=h, m1=m1, rzh=rzh):
                r0 = pl.multiple_of(g * SUBLANES, SUBLANES)
                tile = s1_s[h, pl.ds(r0, SUBLANES), :]
                for r in range(SUBLANES):
                    row = jnp.broadcast_to(tile[r:r + 1, :], (SUBLANES, tm))
                    s1b_s[h, r0 + r] = row
                    e1b_s[h, r0 + r] = jnp.exp(row - m1) * rzh
                return carry

            lax.fori_loop(0, N_KEYS // SUBLANES, bcast_rows, 0)
        acc_s[...] = jnp.zeros(acc_s.shape, F32)

    base = j * ablk
    brows = PEER_BVREGS * SUBLANES
    grows = PEER_AGRP * N_KEYS
    for ag in range(ablk // PEER_AGRP):
        gsl = slice(ag * grows, (ag + 1) * grows)
        act = lax.dot_general(eu_ref[gsl, :], h2_ref[0], NT_DIMS, preferred_element_type=F32)
        for lt in range(tm // LANES):
            ls = slice(lt * LANES, (lt + 1) * LANES)
            for bc in range(N_KEYS // brows):
                rws = [slice(bc * brows + k * SUBLANES, bc * brows + (k + 1) * SUBLANES)
                       for k in range(PEER_BVREGS)]
                g = [[None] * PEER_BVREGS for _ in range(PEER_AGRP)]
                for h in range(H):
                    s2c = [s2_s[h, r, ls] for r in rws]
                    e2c = [e2_s[h, r, ls] for r in rws]
                    tau = taub_s[h, :, ls]
                    for i in range(PEER_AGRP):
                        a = base + ag * PEER_AGRP + i
                        s1a = s1b_s[h, a, :, ls]
                        e1a = e1b_s[h, a, :, ls]
                        for k in range(PEER_BVREGS):
                            term = jnp.where((s1a + s2c[k]) >= tau, e2c[k], 0.0) * e1a
                            g[i][k] = term if h == 0 else g[i][k] + term
                for i in range(PEER_AGRP):
                    r0 = i * N_KEYS + bc * brows
                    gi = jnp.concatenate(g[i], axis=0)
                    wT_s[ag * grows + r0:ag * grows + r0 + brows, ls] = (
                        gi * _gelu(act[r0:r0 + brows, ls])).astype(BF16)
        acc_s[...] += jnp.dot(evT_ref[:, gsl], wT_s[gsl, :], preferred_element_type=F32)

    @pl.when(j == nj - 1)
    def _():
        o_ref[0] = x_ref[0] + gt_ref[0] * acc_s[...].T


def _peer(h2, x1, gt, W, tm, ablk=16):
    G, Tg, D = x1.shape
    n_exp = N_KEYS * N_KEYS
    eb = ablk * N_KEYS
    tok = lambda w: pl.BlockSpec((1, tm, w), lambda g_, i, j: (g_, i, 0))
    if gt.shape[1] == 1:
        gt_spec = pl.BlockSpec((1, 1, D), lambda g_, i, j: (g_, 0, 0))
    else:
        gt_spec = tok(D)
    H = PEER_HEADS
    return pl.pallas_call(
        functools.partial(_peer_kernel, tm=tm, ablk=ablk),
        grid=(G, Tg // tm, n_exp // eb),
        in_specs=[tok(D), tok(D), gt_spec,
                  _resident((D, H * 2 * PEER_HALF)), _resident((H, 2, N_KEYS, PEER_HALF)),
                  pl.BlockSpec((eb, D), lambda g_, i, j: (j, 0)),
                  pl.BlockSpec((D, eb), lambda g_, i, j: (0, j))],
        out_specs=tok(D),
        out_shape=jax.ShapeDtypeStruct((G, Tg, D), F32),
        scratch_shapes=[pltpu.VMEM((H, N_KEYS, tm), F32), pltpu.VMEM((H, N_KEYS, tm), F32),
                        pltpu.VMEM((H, N_KEYS, tm), F32),
                        pltpu.VMEM((H, N_KEYS, SUBLANES, tm), F32), pltpu.VMEM((H, N_KEYS, SUBLANES, tm), F32),
                        pltpu.VMEM((H, SUBLANES, tm), F32), pltpu.VMEM((eb, tm), BF16),
                        pltpu.VMEM((D, tm), F32)],
        compiler_params=_cparams(("arbitrary", "arbitrary", "arbitrary")),
        name="peer",
    )(h2, x1, gt, W["w_pq"], W["sub_keys"], W["eu"], W["evT"])


def _pad_heads(w, per_head):
    lead = w.shape[:-1]
    w = w.reshape(lead + (N_HEADS, per_head))
    w = jnp.pad(w, [(0, 0)] * len(lead) + [(0, 0), (0, HEAD_PAD - per_head)])
    return w.reshape(lead + (N_HEADS * HEAD_PAD,))


def _block_diag(w):
    H, d, _ = w.shape
    eye = jnp.eye(H, dtype=w.dtype)
    return (eye[:, None, :, None] * w[:, :, None, :]).reshape(H * d, H * d)


def _prep_layer(l, w_in, g_norm_mix, g_norm_ffn, g_q_lora, w_uq, g_kv_lora, w_uk, w_uv, g_qn, g_kn,
                conv_w, conv_b, w_ra, b_ra, w_ri, b_ri, lru_lambda, gm_ln_g, gm_ln_b, w_s, b_s,
                w_branch, w_out, w_pq, sub_keys, expert_u, expert_v):
    D = D_MODEL
    wi = w_in[l]
    c0 = Q_LORA + KV_LORA
    zc = lambda n: jnp.zeros((D, n), F32)
    w_in_p = jnp.concatenate([wi[:, :c0], zc(QK_NOPE), wi[:, c0:c0 + QK_ROPE], zc(HEAD_PAD - QK_DIM),
                              wi[:, c0 + QK_ROPE:]], axis=1).astype(BF16)
    w_uk_pf = _pad_heads(w_uk[l], QK_NOPE)
    pad1 = lambda g: jnp.pad(g, (0, HEAD_PAD - g.shape[0])).reshape(1, HEAD_PAD)
    W = dict(
        w_in_p=w_in_p,
        g_mix=g_norm_mix[l].reshape(1, D), g_ffn=g_norm_ffn[l].reshape(1, D),
        g_q=g_q_lora[l].reshape(1, Q_LORA), g_kv=g_kv_lora[l].reshape(1, KV_LORA),
        w_uq_p=_pad_heads(w_uq[l], QK_DIM).astype(BF16),
        w_uk_p=w_uk_pf.astype(BF16), w_uk_pf=w_uk_pf,
        w_ukT=w_uk[l].T.astype(BF16),
        w_uv_p=_pad_heads(w_uv[l], V_DIM).astype(BF16),
        g_qn_p=pad1(g_qn[l]), g_kn_p=pad1(g_kn[l]),
        conv_w=conv_w[l], conv_b=conv_b[l].reshape(1, -1),
        w_ra_bd=_block_diag(w_ra[l]).astype(BF16), b_ra=b_ra[l].reshape(1, -1),
        w_ri_bd=_block_diag(w_ri[l]).astype(BF16), b_ri=b_ri[l].reshape(1, -1),
        lam=lru_lambda[l].reshape(1, -1),
        gm_g=gm_ln_g[l].reshape(1, -1), gm_b=gm_ln_b[l].reshape(1, -1),
        w_br_a=_pad_heads(w_branch[l, 0].T, V_DIM).T.astype(BF16),
        w_br_b=w_branch[l, 1].astype(BF16), w_br_c=w_branch[l, 2].astype(BF16),
        w_out=w_out[l].astype(BF16),
        w_pq=w_pq[l].astype(BF16), sub_keys=sub_keys[l].astype(BF16),
        eu=expert_u[l].astype(BF16), evT=expert_v[l].T.astype(BF16),
    )
    ws = w_s[l]
    bs = b_s[l]
    tril = jnp.tril(ws)
    W["gm_wcat_p"] = tril.transpose(1, 0, 2).reshape(CHUNK, GM_GROUPS * CHUNK).astype(BF16)
    W["gm_bias_p"] = jnp.repeat(bs.T, GM_GD, axis=1)
    return W, ws, bs


def _gmlp_sample_mats(ws, bs, ts):
    reps = CHUNK // ts
    small = jnp.tril(ws[:, :ts, :ts])
    eye = jnp.eye(reps, dtype=F32)
    big = (eye[None, :, None, :, None] * small[:, None, :, None, :]).reshape(GM_GROUPS, CHUNK, CHUNK)
    wcat = big.transpose(1, 0, 2).reshape(CHUNK, GM_GROUPS * CHUNK).astype(BF16)
    bias = jnp.tile(jnp.repeat(bs[:, :ts].T, GM_GD, axis=1), (reps, 1))
    return wcat, bias


def _rope_tables(pos):
    half = QK_ROPE // 2
    freq = ROPE_THETA ** (-jnp.arange(half, dtype=F32) / half)
    ang = pos.astype(F32)[:, None] * freq[None, :]
    cos, sin = jnp.cos(ang), jnp.sin(ang)
    n = pos.shape[0]
    z = lambda w: jnp.zeros((n, w), F32)
    c = jnp.concatenate([jnp.ones((n, QK_NOPE), F32), cos, cos, z(HEAD_PAD - QK_DIM)], axis=1)
    sa = jnp.concatenate([z(QK_NOPE + half), sin, z(HEAD_PAD - QK_DIM)], axis=1)
    sb = jnp.concatenate([z(QK_NOPE), -sin, z(half + HEAD_PAD - QK_DIM)], axis=1)
    return c, sa, sb


def _split_mod(mod):
    return [mod[..., k * D_MODEL:(k + 1) * D_MODEL] for k in range(6)]


def _layer_group(W, x, mods, tabs, attend, buf0, h0, gm_mats, tm, lru_bt, q_dtype):
    G, Tg, D = x.shape
    sh1, sc1, gt1, sh2, sc2, gt2 = mods
    zq, zkvr, zxy, zuv, zg = _inproj(x, sc1, sh1, W["g_mix"], W["w_in_p"], tm)
    q, k, v, rows = _qkprep(zq, zkvr, tabs, W, tm, q_dtype)
    o_a = attend(q, k, v, rows)
    B, T = buf0.shape[0], (G * Tg) // buf0.shape[0]
    zxy_b = zxy.reshape(B, T, 2 * LRU_WIDTH)
    bb, tt = lru_bt
    o_b, h_last = _lru(zxy_b, buf0, h0, W, bb, tt)
    conv_new = zxy_b[:, T - (CONV_WIDTH - 1):, :LRU_WIDTH]
    o_c, vn = _gmlp(zuv.reshape(G * Tg, 2 * GM_WIDTH), gm_mats[0], gm_mats[1], W)
    x1, h2 = _merge(o_a, o_b.reshape(G, Tg, -1), o_c.reshape(G, Tg, -1), zg, x, gt1, sc2, sh2, W, tm)
    xo = _peer(h2, x1, gt2, W, tm)
    return xo, rows, h_last.reshape(B, LRU_WIDTH), conv_new, vn


def kernel(x_prompt, x_sample, c_prompt, c_sample, cache_kv, state_lru_h, state_conv, page_table, w_ada, b_ada, g_norm_mix, g_norm_ffn, w_in, g_q_lora, w_uq, g_kv_lora, w_uk, w_uv, g_qn, g_kn, conv_w, conv_b, w_ra, b_ra, w_ri, b_ri, lru_lambda, gm_ln_g, gm_ln_b, w_s, b_s, w_branch, w_out, w_pq, sub_keys, expert_u, expert_v):
    Bp, Tp, D = x_prompt.shape
    Bs, Ts, _ = x_sample.shape
    n_past = page_table.shape[1] * cache_kv.shape[2]
    TM = 256
    cache_t = jnp.swapaxes(cache_kv, 2, 3)

    c_all = jnp.concatenate([c_prompt, c_sample], axis=0)
    nb = c_all.shape[0]
    nb_pad = -(-nb // SUBLANES) * SUBLANES
    c_all = jnp.pad(c_all, ((0, nb_pad - nb), (0, 0)))
    mod_all = _ada(c_all, w_ada, b_ada)

    tabs_p = _rope_tables(jnp.arange(Tp, dtype=F32))
    tabs_s = tuple(jnp.tile(t, (Bs, 1)) for t in _rope_tables(n_past + jnp.arange(Ts, dtype=F32)))

    xp = x_prompt
    xs = x_sample.reshape(1, Bs * Ts, D)
    buf0_p = jnp.zeros((Bp, CONV_WIDTH - 1, LRU_WIDTH), F32)
    h0_p = jnp.zeros((Bp, 1, LRU_WIDTH), F32)
    outs = [[] for _ in range(8)]
    for l in range(DEPTH):
        W, ws, bs = _prep_layer(l, w_in, g_norm_mix, g_norm_ffn, g_q_lora, w_uq, g_kv_lora, w_uk, w_uv,
                                g_qn, g_kn, conv_w, conv_b, w_ra, b_ra, w_ri, b_ri, lru_lambda, gm_ln_g,
                                gm_ln_b, w_s, b_s, w_branch, w_out, w_pq, sub_keys, expert_u, expert_v)
        mods_p = [m[:, None, :] for m in _split_mod(mod_all[l, :Bp])]
        mods_s = [jnp.repeat(m, Ts, axis=0)[None] for m in _split_mod(mod_all[l, Bp:Bp + Bs])]

        def attend_p(q, k, v, rows):
            return _flash(q, k, v)

        def attend_s(q, k, v, rows, W=W, l=l):
            qabs, qg = _qabs(q.reshape(Bs * Ts, -1), W)
            qabs = qabs.reshape(Bs, Ts, N_HEADS, KV_LORA).transpose(0, 2, 1, 3)
            qabs = qabs.reshape(Bs, N_HEADS * Ts, KV_LORA).astype(BF16)
            qr = qg.reshape(Bs, Ts, N_HEADS, HEAD_PAD)[..., QK_NOPE:QK_DIM].transpose(0, 2, 1, 3)
            qr = qr.reshape(Bs, N_HEADS * Ts, QK_ROPE).astype(BF16)
            o = _paged(page_table, qabs, qr, rows.reshape(Bs, Ts, KV_ROW), cache_t, W, l)
            return o.reshape(1, Bs * Ts, -1)

        xp, rp, hp, cp, vp = _layer_group(W, xp, mods_p, tabs_p, attend_p, buf0_p, h0_p,
                                          (W["gm_wcat_p"], W["gm_bias_p"]), TM, (Bp, 256), BF16)
        xs, rs, hs, cs, vs = _layer_group(W, xs, mods_s, tabs_s, attend_s, state_conv[:, l],
                                          state_lru_h[:, l][:, None, :], _gmlp_sample_mats(ws, bs, Ts),
                                          TM, (16, Ts), F32)
        outs[0].append(rp)
        outs[1].append(rs.reshape(Bs, Ts, KV_ROW))
        outs[2].append(hp)
        outs[3].append(hs)
        outs[4].append(cp)
        outs[5].append(cs)
        outs[6].append(vp.reshape(Bp, Tp, GM_WIDTH)[:, -CHUNK:])
        outs[7].append(vs.reshape(Bs, Ts, GM_WIDTH))
    stacked = [jnp.stack(o, axis=1) for o in outs]
    return (xp, xs.reshape(Bs, Ts, D), *stacked)
```
